```python
import jax, jax.numpy as jnp
from jax import lax
import numpy as np

D_MODEL = 2048
BATCH = 2
SEQ = 4096
DEPTH = 2
DEC_BATCH = 32
DEC_SEQ = 16
PAST_LEN = 4096

CHUNK = 64
N_MIXERS = 2
N_A = (DEPTH + 1) // 2
N_B = DEPTH // 2
N_HEADS = 16
N_KV_HEADS = 4
GROUP = N_HEADS // N_KV_HEADS
HEAD_DIM = D_MODEL // N_HEADS
ROT_DIM = HEAD_DIM // 4
IDX_HEADS = 16
IDX_DIM = 64
IDX_ROT_DIM = IDX_DIM // 4
TOPK_MAX = 256
ROPE_THETA = 500000.0
D_FF = 5632
CONV_W = 3
QBLK = 128
EPS = 1e-6
Q_W = N_HEADS * HEAD_DIM
KV_W = N_KV_HEADS * HEAD_DIM
IQ_W = IDX_HEADS * IDX_DIM
DSA_SPLITS = [Q_W, Q_W + KV_W, Q_W + 2 * KV_W, Q_W + 2 * KV_W + IQ_W, Q_W + 2 * KV_W + IQ_W + IDX_DIM]
DSA_IN = Q_W + 2 * KV_W + IQ_W + IDX_DIM + IDX_HEADS
FOX_SPLITS = [Q_W, Q_W + KV_W, Q_W + 2 * KV_W]
FOX_IN = Q_W + 2 * KV_W + N_HEADS

kernel_name = 'hybrid_dsa_fox_convffn_stream_step'


def rms_norm(x, g):
    xf = x.astype(jnp.float32)
    y = xf * lax.rsqrt(jnp.mean(xf * xf, axis=-1, keepdims=True) + EPS)
    return (y * g.astype(jnp.float32)).astype(x.dtype)


def partial_rope(x, pos, rot_dim):
    half = rot_dim // 2
    inv = ROPE_THETA ** (-jnp.arange(half, dtype=jnp.float32) * 2.0 / rot_dim)
    ang = (pos[:, None] * inv[None, :]).reshape((1, pos.shape[0]) + (1,) * (x.ndim - 3) + (half,))
    cos, sin = jnp.cos(ang), jnp.sin(ang)
    xf = x.astype(jnp.float32)
    x1, x2, rest = xf[..., :half], xf[..., half:rot_dim], xf[..., rot_dim:]
    out = jnp.concatenate([x1 * cos - x2 * sin, x2 * cos + x1 * sin, rest], axis=-1)
    return out.astype(x.dtype)


def to_blocks(a, blk):
    b, t = a.shape[:2]
    return jnp.swapaxes(a.reshape((b, t // blk, blk) + a.shape[2:]), 0, 1)


def from_blocks(a):
    a = jnp.swapaxes(a, 0, 1)
    return a.reshape((a.shape[0], a.shape[1] * a.shape[2]) + a.shape[3:])


def dsa_mixer(h, w_in, w_o, past_k, past_v, past_ik):
    B, T, _ = h.shape
    P = past_k.shape[1]
    L = P + T
    n_sel = min(TOPK_MAX, L // 4)
    qpos = P + jnp.arange(T, dtype=jnp.int32)
    qpos_f = qpos.astype(jnp.float32)
    kchunk = jnp.arange(L, dtype=jnp.int32) // CHUNK
    q, k, v, iq, ik, iw = jnp.split(h @ w_in, DSA_SPLITS, axis=-1)
    q = partial_rope(q.reshape(B, T, N_HEADS, HEAD_DIM), qpos_f, ROT_DIM)
    k = partial_rope(k.reshape(B, T, N_KV_HEADS, HEAD_DIM), qpos_f, ROT_DIM)
    v = v.reshape(B, T, N_KV_HEADS, HEAD_DIM)
    iq = partial_rope(iq.reshape(B, T, IDX_HEADS, IDX_DIM), qpos_f, IDX_ROT_DIM)
    ik = partial_rope(ik, qpos_f, IDX_ROT_DIM)
    iw = iw.astype(jnp.float32) * (IDX_HEADS ** -0.5 * IDX_DIM ** -0.5)
    k_all = jnp.concatenate([past_k, k], axis=1)
    v_all = jnp.concatenate([past_v, v], axis=1)
    ik_all = jnp.concatenate([past_ik, ik], axis=1)
    blk = QBLK if T % QBLK == 0 else T
    gather = jax.vmap(lambda rows, idx: rows[idx])

    def block(args):
        qb, iqb, iwb, qposb = args
        s = jnp.einsum('bthd,bsd->bths', iqb, ik_all, preferred_element_type=jnp.float32)
        score = jnp.einsum('bths,bth->bts', jax.nn.relu(s), iwb)
        adm = kchunk[None, :] <= (qposb // CHUNK)[:, None]
        score = jnp.where(adm[None], score, -jnp.inf)
        top_val, top_idx = lax.top_k(score, n_sel)
        valid = jnp.isfinite(top_val)
        ks = gather(k_all, top_idx)
        vs = gather(v_all, top_idx)
        qg = qb.reshape(B, blk, N_KV_HEADS, GROUP, HEAD_DIM)
        logits = jnp.einsum('btngd,btknd->btngk', qg, ks, preferred_element_type=jnp.float32) * (HEAD_DIM ** -0.5)
        logits = jnp.where(valid[:, :, None, None, :], logits, -jnp.inf)
        p = jax.nn.softmax(logits, axis=-1).astype(vs.dtype)
        o = jnp.einsum('btngk,btknd->btngd', p, vs, preferred_element_type=jnp.float32)
        return o.reshape(B, blk, Q_W).astype(h.dtype)

    out = lax.map(block, (to_blocks(q, blk), to_blocks(iq, blk), to_blocks(iw, blk), qpos.reshape(-1, blk)))
    return from_blocks(out) @ w_o, k, v, ik


def fox_mixer(h, w_in, b_f, w_o, past_k, past_v, past_logf):
    B, T, _ = h.shape
    P = past_k.shape[1]
    L = P + T
    qpos = P + jnp.arange(T, dtype=jnp.int32)
    kpos = jnp.arange(L, dtype=jnp.int32)
    q, k, v, fg = jnp.split(h @ w_in, FOX_SPLITS, axis=-1)
    q = q.reshape(B, T, N_HEADS, HEAD_DIM)
    k = k.reshape(B, T, N_KV_HEADS, HEAD_DIM)
    v = v.reshape(B, T, N_KV_HEADS, HEAD_DIM)
    logf = jax.nn.log_sigmoid(fg.astype(jnp.float32) + b_f.astype(jnp.float32)).astype(h.dtype)
    k_all = jnp.concatenate([past_k, k], axis=1)
    v_all = jnp.concatenate([past_v, v], axis=1)
    cum = jnp.cumsum(jnp.concatenate([past_logf, logf], axis=1).astype(jnp.float32), axis=1)
    ck = cum.reshape(B, L, N_KV_HEADS, GROUP).transpose(0, 2, 3, 1)
    cq = cum[:, P:]
    blk = QBLK if T % QBLK == 0 else T

    def block(args):
        qb, cqb, qposb = args
        qg = qb.reshape(B, blk, N_KV_HEADS, GROUP, HEAD_DIM)
        logits = jnp.einsum('btngd,bsnd->bngts', qg, k_all, preferred_element_type=jnp.float32) * (HEAD_DIM ** -0.5)
        bias = cqb.reshape(B, blk, N_KV_HEADS, GROUP).transpose(0, 2, 3, 1)[..., None] - ck[:, :, :, None, :]
        causal = kpos[None, :] <= qposb[:, None]
        logits = jnp.where(causal, logits + bias, -jnp.inf)
        p = jax.nn.softmax(logits, axis=-1).astype(v_all.dtype)
        o = jnp.einsum('bngts,bsnd->btngd', p, v_all, preferred_element_type=jnp.float32)
        return o.reshape(B, blk, Q_W).astype(h.dtype)

    out = lax.map(block, (to_blocks(q, blk), to_blocks(cq, blk), qpos.reshape(-1, blk)))
    return from_blocks(out) @ w_o, k, v, logf


def conv_ffn(h, w_up, conv_w, conv_b, w_down, past_u):
    T = h.shape[1]
    u = h @ w_up
    u_all = jnp.concatenate([past_u, u], axis=1)
    z = conv_b
    for j in range(CONV_W):
        z = z + u_all[:, j:j + T] * conv_w[j]
    gate, val = jnp.split(z, 2, axis=-1)
    return (jax.nn.silu(gate) * val) @ w_down, u_all[:, T:]


def trunk(x, c, past_dsa_k, past_dsa_v, past_dsa_ik, past_fox_k, past_fox_v, past_fox_logf, past_ffn_u,
          norm_mix_g, norm_ffn_g, w_ada, b_ada, dsa_w_in, dsa_w_o, fox_w_in, fox_b_f, fox_w_o,
          ffn_w_up, ffn_conv_w, ffn_conv_b, ffn_w_down, final_g):
    ka, va, ika, kb, vb, lfb, us = [], [], [], [], [], [], []
    cs = jax.nn.silu(c)
    for l in range(DEPTH):
        mod = cs @ w_ada[l] + b_ada[l]
        sh_m, sc_m, g_m, sh_f, sc_f, g_f = [m[:, None, :] for m in jnp.split(mod, 6, axis=-1)]
        h = rms_norm(x, norm_mix_g[l]) * (1 + sc_m) + sh_m
        j = l // N_MIXERS
        if l % N_MIXERS == 0:
            y, k, v, ik = dsa_mixer(h, dsa_w_in[j], dsa_w_o[j], past_dsa_k[j], past_dsa_v[j], past_dsa_ik[j])
            ka.append(k); va.append(v); ika.append(ik)
        else:
            y, k, v, lf = fox_mixer(h, fox_w_in[j], fox_b_f[j], fox_w_o[j], past_fox_k[j], past_fox_v[j], past_fox_logf[j])
            kb.append(k); vb.append(v); lfb.append(lf)
        x = x + g_m * y
        h = rms_norm(x, norm_ffn_g[l]) * (1 + sc_f) + sh_f
        y, u_state = conv_ffn(h, ffn_w_up[l], ffn_conv_w[l], ffn_conv_b[l], ffn_w_down[l], past_ffn_u[l])
        us.append(u_state)
        x = x + g_f * y
    return (rms_norm(x, final_g), jnp.stack(ka), jnp.stack(va), jnp.stack(ika),
            jnp.stack(kb), jnp.stack(vb), jnp.stack(lfb), jnp.stack(us))


def setup_inputs(seed: int = 0) -> dict:
    key = jax.random.key(seed)
    keys = iter(jax.random.split(key, 40))
    nrm = lambda shape, s=1.0: jax.random.normal(next(keys), shape, jnp.float32) * s
    d = D_MODEL
    return {
        'x_prompt': nrm((BATCH, SEQ, d)),
        'x_sample': nrm((DEC_BATCH, DEC_SEQ, d)),
        'c_prompt': nrm((BATCH, d)),
        'c_sample': nrm((DEC_BATCH, d)),
        'cache_dsa_k': nrm((N_A, DEC_BATCH, PAST_LEN, N_KV_HEADS, HEAD_DIM)),
        'cache_dsa_v': nrm((N_A, DEC_BATCH, PAST_LEN, N_KV_HEADS, HEAD_DIM)),
        'cache_dsa_idx_k': nrm((N_A, DEC_BATCH, PAST_LEN, IDX_DIM)),
        'cache_fox_k': nrm((N_B, DEC_BATCH, PAST_LEN, N_KV_HEADS, HEAD_DIM)),
        'cache_fox_v': nrm((N_B, DEC_BATCH, PAST_LEN, N_KV_HEADS, HEAD_DIM)),
        'cache_fox_logf': jax.nn.log_sigmoid(2.0 + nrm((N_B, DEC_BATCH, PAST_LEN, N_HEADS))),
        'state_ffn_conv': nrm((DEPTH, DEC_BATCH, CONV_W - 1, 2 * D_FF)),
        'norm_mix_g': 1.0 + nrm((DEPTH, d), 0.1),
        'norm_ffn_g': 1.0 + nrm((DEPTH, d), 0.1),
        'w_ada': nrm((DEPTH, d, 6 * d), d ** -0.5),
        'b_ada': nrm((DEPTH, 6 * d), 0.01),
        'dsa_w_in': nrm((N_A, d, DSA_IN), d ** -0.5),
        'dsa_w_o': nrm((N_A, Q_W, d), Q_W ** -0.5),
        'fox_w_in': nrm((N_B, d, FOX_IN), d ** -0.5),
        'fox_b_f': 2.0 + nrm((N_B, N_HEADS), 0.5),
        'fox_w_o': nrm((N_B, Q_W, d), Q_W ** -0.5),
        'ffn_w_up': nrm((DEPTH, d, 2 * D_FF), d ** -0.5),
        'ffn_conv_w': nrm((DEPTH, CONV_W, 2 * D_FF), CONV_W ** -0.5),
        'ffn_conv_b': nrm((DEPTH, 2 * D_FF), 0.01),
        'ffn_w_down': nrm((DEPTH, D_FF, d), D_FF ** -0.5),
        'final_g': 1.0 + nrm((d,), 0.1),
    }


def reference(x_prompt, x_sample, c_prompt, c_sample, cache_dsa_k, cache_dsa_v, cache_dsa_idx_k,
              cache_fox_k, cache_fox_v, cache_fox_logf, state_ffn_conv,
              norm_mix_g, norm_ffn_g, w_ada, b_ada, dsa_w_in, dsa_w_o, fox_w_in, fox_b_f, fox_w_o,
              ffn_w_up, ffn_conv_w, ffn_conv_b, ffn_w_down, final_g):
    dt = x_prompt.dtype
    b = x_prompt.shape[0]
    (y_prompt, dsa_k_p, dsa_v_p, dsa_ik_p, fox_k_p, fox_v_p, fox_lf_p, ffn_u_p) = trunk(
        x_prompt, c_prompt,
        jnp.zeros((N_A, b, 0, N_KV_HEADS, HEAD_DIM), dt), jnp.zeros((N_A, b, 0, N_KV_HEADS, HEAD_DIM), dt),
        jnp.zeros((N_A, b, 0, IDX_DIM), dt),
        jnp.zeros((N_B, b, 0, N_KV_HEADS, HEAD_DIM), dt), jnp.zeros((N_B, b, 0, N_KV_HEADS, HEAD_DIM), dt),
        jnp.zeros((N_B, b, 0, N_HEADS), dt),
        jnp.zeros((DEPTH, b, CONV_W - 1, 2 * D_FF), dt),
        norm_mix_g, norm_ffn_g, w_ada, b_ada, dsa_w_in, dsa_w_o, fox_w_in, fox_b_f, fox_w_o,
        ffn_w_up, ffn_conv_w, ffn_conv_b, ffn_w_down, final_g)
    (y_sample, dsa_k_s, dsa_v_s, dsa_ik_s, fox_k_s, fox_v_s, fox_lf_s, ffn_u_s) = trunk(
        x_sample, c_sample, cache_dsa_k, cache_dsa_v, cache_dsa_idx_k,
        cache_fox_k, cache_fox_v, cache_fox_logf, state_ffn_conv,
        norm_mix_g, norm_ffn_g, w_ada, b_ada, dsa_w_in, dsa_w_o, fox_w_in, fox_b_f, fox_w_o,
        ffn_w_up, ffn_conv_w, ffn_conv_b, ffn_w_down, final_g)
    return (y_prompt, y_sample,
            dsa_k_p, dsa_v_p, dsa_ik_p, fox_k_p, fox_v_p, fox_lf_p, ffn_u_p,
            dsa_k_s, dsa_v_s, dsa_ik_s, fox_k_s, fox_v_s, fox_lf_s, ffn_u_s)
```

```python
import functools

import jax
import jax.numpy as jnp
import numpy as np
from jax import lax
from jax.experimental import pallas as pl
from jax.experimental.pallas import tpu as pltpu

F32 = jnp.float32
BF16 = jnp.bfloat16

CHUNK = 64
N_MIXERS = 2
N_HEADS = 16
N_KV_HEADS = 4
GROUP = N_HEADS // N_KV_HEADS
HEAD_DIM = 128
ROT_DIM = HEAD_DIM // 4
IDX_HEADS = 16
IDX_DIM = 64
IDX_ROT_DIM = IDX_DIM // 4
TOPK_MAX = 256
ROPE_THETA = 500000.0
CONV_W = 3
EPS = 1e-6
Q_W = N_HEADS * HEAD_DIM
KV_W = N_KV_HEADS * HEAD_DIM
IQ_W = IDX_HEADS * IDX_DIM

LANES = 128
NEG = -1e30
INT_MIN = -(2 ** 31)
VMEM_LIMIT = 56 * 2 ** 20


def _cparams(sem):
    return pltpu.CompilerParams(dimension_semantics=sem, vmem_limit_bytes=VMEM_LIMIT)


def _pick(n, cands):
    for c in cands:
        if n % c == 0:
            return c
    return n


def _ada_kernel(c_ref, w_ref, b_ref, o_ref):
    c = c_ref[...]
    cs = (c * jax.nn.sigmoid(c)).astype(BF16)
    o_ref[0] = jnp.dot(cs, w_ref[0].astype(BF16), preferred_element_type=F32) + b_ref[0]


def _ada_call(c_all, w_ada, b_ada):
    depth, d, n6 = w_ada.shape
    bc = c_all.shape[0]
    tn = _pick(n6, (1024, 512, 256, 128))
    return pl.pallas_call(
        _ada_kernel,
        grid=(depth, n6 // tn),
        in_specs=[pl.BlockSpec((bc, d), lambda l, j: (0, 0)),
                  pl.BlockSpec((1, d, tn), lambda l, j: (l, 0, j)),
                  pl.BlockSpec((1, 1, tn), lambda l, j: (l, 0, j))],
        out_specs=pl.BlockSpec((1, bc, tn), lambda l, j: (l, 0, j)),
        out_shape=jax.ShapeDtypeStruct((depth, bc, n6), F32),
        compiler_params=_cparams(("arbitrary", "arbitrary")),
        name="ada_mod",
    )(c_all, w_ada, b_ada.reshape(depth, 1, n6))


def _mod_spec(arr3, m, tm, tn, col_of):
    g, r, _ = arr3.shape
    if r == 1:
        bpg = (m // g) // tm
        return pl.BlockSpec((1, 1, tn), lambda *ix: (ix[0] // bpg, 0, col_of(*ix)))
    assert g == 1 and r == m
    return pl.BlockSpec((1, tm, tn), lambda *ix: (0, ix[0], col_of(*ix)))


def _norm_kernel(modulate, x_ref, g_ref, *refs):
    x = x_ref[...]
    y = x * lax.rsqrt(jnp.mean(x * x, axis=-1, keepdims=True) + EPS) * g_ref[0]
    if modulate:
        sc_ref, sh_ref, o_ref = refs
        y = y * (1.0 + sc_ref[0]) + sh_ref[0]
    else:
        (o_ref,) = refs
    o_ref[...] = y.astype(o_ref.dtype)


def _norm_call(x, g3, l, sc3, sh3, out_dtype):
    m, d = x.shape
    tm = _pick(m, (512, 256, 128))
    in_specs = [pl.BlockSpec((tm, d), lambda i: (i, 0)),
                pl.BlockSpec((1, 1, d), lambda i: (l, 0, 0))]
    args = [x, g3]
    if sc3 is not None:
        in_specs += [_mod_spec(sc3, m, tm, d, lambda i: 0), _mod_spec(sh3, m, tm, d, lambda i: 0)]
        args += [sc3, sh3]
    return pl.pallas_call(
        functools.partial(_norm_kernel, sc3 is not None),
        grid=(m // tm,),
        in_specs=in_specs,
        out_specs=pl.BlockSpec((tm, d), lambda i: (i, 0)),
        out_shape=jax.ShapeDtypeStruct((m, d), out_dtype),
        compiler_params=_cparams(("arbitrary",)),
        name="rms_norm",
    )(*args)


def _mm_kernel(epilogue, nk, n_extra, a_ref, w_ref, *refs):
    extra = refs[:n_extra]
    if nk == 1:
        outs = refs[n_extra:]
        epilogue(jnp.dot(a_ref[...], w_ref[0].astype(BF16), preferred_element_type=F32), extra, outs)
        return
    outs, acc_ref = refs[n_extra:-1], refs[-1]
    k = pl.program_id(2)
    part = jnp.dot(a_ref[...], w_ref[0].astype(BF16), preferred_element_type=F32)

    @pl.when(k == 0)
    def _():
        acc_ref[...] = part

    @pl.when(k > 0)
    def _():
        acc_ref[...] += part

    @pl.when(k == nk - 1)
    def _():
        epilogue(acc_ref[...], extra, outs)


def _linear_call(name, epilogue, a, w3, l, col0, ncols, tn, tm, tk, extras, out_dtypes):
    m, kdim = a.shape
    assert col0 % tn == 0 and m % tm == 0 and kdim % tk == 0
    cb0, ncb, nk = col0 // tn, -(-ncols // tn), kdim // tk
    in_specs = [pl.BlockSpec((tm, tk), lambda i, j, k: (i, k)),
                pl.BlockSpec((1, tk, tn), lambda i, j, k: (l, k, cb0 + j))]
    in_specs += [s for _, s in extras]
    outs = [jax.ShapeDtypeStruct((m, ncb * tn), dt) for dt in out_dtypes]
    out_specs = [pl.BlockSpec((tm, tn), lambda i, j, k: (i, j)) for _ in out_dtypes]
    scratch = [pltpu.VMEM((tm, tn), F32)] if nk > 1 else []
    return pl.pallas_call(
        functools.partial(_mm_kernel, epilogue, nk, len(extras)),
        grid=(m // tm, ncb, nk),
        in_specs=in_specs,
        out_specs=out_specs,
        out_shape=outs,
        scratch_shapes=scratch,
        compiler_params=_cparams(("arbitrary", "arbitrary", "arbitrary")),
        name=name,
    )(a, w3, *[x for x, _ in extras])


def _lane_tile(x, reps):
    return x if reps == 1 else jnp.concatenate([x] * reps, axis=1)


def _ep_rope(shift, acc, extra, outs):
    (tab_ref,) = extra
    n = acc.shape[1]
    reps = n // LANES
    a = _lane_tile(tab_ref[:, 0:LANES], reps)
    bm = _lane_tile(tab_ref[:, LANES:2 * LANES], reps)
    bp = _lane_tile(tab_ref[:, 2 * LANES:3 * LANES], reps)
    y = acc * a + pltpu.roll(acc, n - shift, 1) * bm + pltpu.roll(acc, shift, 1) * bp
    for o in outs:
        o[...] = y.astype(o.dtype)


def _ep_scale(scale, acc, extra, outs):
    for o in outs:
        o[...] = (acc * scale).astype(o.dtype)


def _ep_plain(acc, extra, outs):
    for o in outs:
        o[...] = acc.astype(o.dtype)


def _ep_logsig(acc, extra, outs):
    (b_ref,) = extra
    x = acc + b_ref[...]
    y = jnp.minimum(x, 0.0) - jnp.log1p(jnp.exp(-jnp.abs(x)))
    outs[0][...] = y


def _ep_residual(acc, extra, outs):
    x_ref, g_ref = extra
    outs[0][...] = x_ref[...] + g_ref[0] * acc


def _rope_table(pos, rot_dim, period, scale=1.0, const_lanes=None):
    half = rot_dim // 2
    inv = ROPE_THETA ** (-jnp.arange(half, dtype=F32) * 2.0 / rot_dim)
    ang = pos[:, None] * inv[None, :]
    cos, sin = jnp.cos(ang), jnp.sin(ang)
    lane = np.arange(LANES) % period
    idx = lane % half
    lo = jnp.asarray(lane < half)[None, :]
    hi = jnp.asarray((lane >= half) & (lane < rot_dim))[None, :]
    a = jnp.where(lo | hi, cos[:, idx], 1.0)
    bm = jnp.where(lo, -sin[:, idx], 0.0)
    bp = jnp.where(hi, sin[:, idx], 0.0)
    if const_lanes is not None:
        a, bm, bp = const_lanes(a, bm, bp)
    return jnp.concatenate([a, bm, bp], axis=1) * scale


def _split3_dot(x, u):
    hi = x.astype(BF16)
    r1 = x - hi.astype(F32)
    mid = r1.astype(BF16)
    lo = (r1 - mid.astype(F32)).astype(BF16)
    d = lambda p: jnp.dot(p, u, preferred_element_type=F32)
    return d(hi) + (d(mid) + d(lo))


def _cumsum_kernel(nb, x_ref, o_ref):
    r = lax.broadcasted_iota(jnp.int32, (LANES, LANES), 0)
    c = lax.broadcasted_iota(jnp.int32, (LANES, LANES), 1)
    u = jnp.where(r <= c, 1.0, 0.0).astype(BF16)
    carry = jnp.zeros((x_ref.shape[0], 1), F32)
    for b in range(nb):
        y = _split3_dot(x_ref[:, b * LANES:(b + 1) * LANES], u) + carry
        o_ref[:, b * LANES:(b + 1) * LANES] = y
        carry = y[:, LANES - 1:LANES]


def _cumsum_call(x):
    r, length = x.shape
    tr = _pick(r, (256, 128, 64, 32, 16, 8))
    return pl.pallas_call(
        functools.partial(_cumsum_kernel, length // LANES),
        grid=(r // tr,),
        in_specs=[pl.BlockSpec((tr, length), lambda i: (i, 0))],
        out_specs=pl.BlockSpec((tr, length), lambda i: (i, 0)),
        out_shape=jax.ShapeDtypeStruct((r, length), F32),
        compiler_params=_cparams(("arbitrary",)),
        name="cumsum",
    )(x)


def _nt_dot(a, b):
    return lax.dot_general(a, b, (((1,), (1,)), ((), ())), preferred_element_type=F32)


def _flash_init(m_ref, l_ref, acc_ref):
    m_ref[...] = jnp.full(m_ref.shape, NEG, F32)
    l_ref[...] = jnp.zeros(l_ref.shape, F32)
    acc_ref[...] = jnp.zeros(acc_ref.shape, F32)


def _flash_step(slabs, vt, rows, m_ref, l_ref, acc_ref):
    tk = slabs[0].shape[1]
    ps = []
    for g, s in enumerate(slabs):
        sl = slice(g * rows, (g + 1) * rows)
        m_prev = m_ref[sl]
        m_new = jnp.maximum(m_prev, jnp.max(s, axis=1, keepdims=True))
        alpha = jnp.exp(m_prev - m_new)
        p = jnp.exp(s - _lane_tile(m_new, tk // LANES))
        l_ref[sl] = alpha * l_ref[sl] + jnp.sum(p, axis=1, keepdims=True)
        acc_ref[sl] = acc_ref[sl] * alpha
        m_ref[sl] = m_new
        ps.append(p.astype(BF16))
    p_all = ps[0] if len(ps) == 1 else jnp.concatenate(ps, axis=0)
    acc_ref[...] += jnp.dot(p_all, vt, preferred_element_type=F32)


def _flash_out(n, rows, l_ref, acc_ref, o_ref):
    for g in range(GROUP):
        sl = slice(g * rows, (g + 1) * rows)
        h = n * GROUP + g
        o_ref[:, h * HEAD_DIM:(h + 1) * HEAD_DIM] = (acc_ref[sl] / l_ref[sl]).astype(o_ref.dtype)


def _group_queries(q_ref, n):
    return jnp.concatenate(
        [q_ref[:, (n * GROUP + g) * HEAD_DIM:(n * GROUP + g + 1) * HEAD_DIM] for g in range(GROUP)], axis=0)


def _sort_key(score):
    bits = pltpu.bitcast(score, jnp.int32)
    return bits ^ ((bits >> 31) & jnp.int32(0x7FFFFFFF))


def _stack_index_queries(iq_ref, lhs_ref, rows):
    lane = lax.broadcasted_iota(jnp.int32, (rows, LANES), 1)
    zero = jnp.zeros((rows, LANES), BF16)
    for p in range(IDX_HEADS // 2):
        pair = iq_ref[:, p * LANES:(p + 1) * LANES]
        lhs_ref[(2 * p) * rows:(2 * p + 1) * rows] = jnp.where(lane < IDX_DIM, pair, zero)
        lhs_ref[(2 * p + 1) * rows:(2 * p + 2) * rows] = jnp.where(lane >= IDX_DIM, pair, zero)


def _index_scores(lhs_ref, iwb_ref, ik2_blk, rows):
    tk = ik2_blk.shape[0]
    s_all = _nt_dot(lhs_ref[...], ik2_blk)
    acc = jnp.zeros((rows, tk), F32)
    for h in range(IDX_HEADS):
        acc = acc + jnp.maximum(s_all[h * rows:(h + 1) * rows], 0.0) * _lane_tile(iwb_ref[h], tk // LANES)
    return acc


def _broadcast_index_weights(tail_ref, iwb_ref, rows):
    tail = tail_ref[...]
    for h in range(IDX_HEADS):
        iwb_ref[h] = jnp.broadcast_to(tail[:, IDX_DIM + h:IDX_DIM + h + 1], (rows, LANES))


def _kth_largest_key(count_ge, rows, n_sel):
    def bit_body(it, t):
        cand = t + jnp.left_shift(jnp.int32(1), 31 - it)
        cnt = count_ge(cand)
        return jnp.where(cnt >= float(n_sel), cand, t)
    t = lax.fori_loop(0, 32, bit_body, jnp.full((rows, LANES), INT_MIN, jnp.int32))
    return jnp.maximum(t, INT_MIN + 1)


def _count_block(key, cand, c):
    for cc in range(key.shape[1] // LANES):
        c = c + jnp.where(key[:, cc * LANES:(cc + 1) * LANES] >= cand, 1.0, 0.0)
    return c


def _lane_total(c):
    return jnp.broadcast_to(jnp.sum(c, axis=1, keepdims=True), c.shape)


def _dsa_prompt_kernel(tq, tk, n_sel, q_ref, iq_ref, tail_ref, k_ref, v_ref, ik2_ref, o_ref,
                       key_ref, bias_ref, lhs_ref, iwb_ref, m_ref, l_ref, acc_ref):
    q0 = pl.program_id(1) * tq
    nkb = (q0 + tq + tk - 1) // tk
    _broadcast_index_weights(tail_ref, iwb_ref, tq)
    _stack_index_queries(iq_ref, lhs_ref, tq)
    row = lax.broadcasted_iota(jnp.int32, (tq, tk), 0)
    col = lax.broadcasted_iota(jnp.int32, (tq, tk), 1)
    lim = ((q0 + row) // CHUNK + 1) * CHUNK

    def score_body(kb, _):
        k0 = pl.multiple_of(kb * tk, tk)
        score = _index_scores(lhs_ref, iwb_ref, ik2_ref[pl.ds(k0, tk), :], tq)
        key_ref[kb] = jnp.where(k0 + col < lim, _sort_key(score), INT_MIN)
        return 0
    lax.fori_loop(0, nkb, score_body, 0)

    def count_ge(cand):
        c = lax.fori_loop(0, nkb, lambda kb, c: _count_block(key_ref[kb], cand, c),
                          jnp.zeros((tq, LANES), F32))
        return _lane_total(c)
    thr = _lane_tile(_kth_largest_key(count_ge, tq, n_sel), tk // LANES)

    def bias_body(kb, _):
        bias_ref[kb] = jnp.where(key_ref[kb] >= thr, 0.0, NEG)
        return 0
    lax.fori_loop(0, nkb, bias_body, 0)

    for n in range(N_KV_HEADS):
        qn = _group_queries(q_ref, n)
        _flash_init(m_ref, l_ref, acc_ref)

        def body(kb, _):
            k0 = pl.multiple_of(kb * tk, tk)
            kt = k_ref[pl.ds(k0, tk), n * HEAD_DIM:(n + 1) * HEAD_DIM]
            vt = v_ref[pl.ds(k0, tk), n * HEAD_DIM:(n + 1) * HEAD_DIM]
            s = _nt_dot(qn, kt)
            b = bias_ref[kb]
            _flash_step([s[g * tq:(g + 1) * tq] + b for g in range(GROUP)], vt, tq, m_ref, l_ref, acc_ref)
            return 0
        lax.fori_loop(0, nkb, body, 0)
        _flash_out(n, tq, l_ref, acc_ref, o_ref)


def _dsa_prompt_call(q, iq, tail, kb16, vb16, ik2, batch, t, n_sel):
    tq = _pick(t, (128,))
    tk = _pick(t, (512, 256, 128))
    nq = t // tq
    row_spec = lambda w: pl.BlockSpec((tq, w), lambda b, i: (b * nq + i, 0))
    full_spec = lambda w: pl.BlockSpec((t, w), lambda b, i: (b, 0))
    return pl.pallas_call(
        functools.partial(_dsa_prompt_kernel, tq, tk, n_sel),
        grid=(batch, nq),
        in_specs=[row_spec(Q_W), row_spec(IQ_W), row_spec(LANES),
                  full_spec(KV_W), full_spec(KV_W), full_spec(LANES)],
        out_specs=row_spec(Q_W),
        out_shape=jax.ShapeDtypeStruct((batch * t, Q_W), BF16),
        scratch_shapes=[pltpu.VMEM((t // tk, tq, tk), jnp.int32),
                        pltpu.VMEM((t // tk, tq, tk), F32),
                        pltpu.VMEM((IDX_HEADS * tq, LANES), BF16),
                        pltpu.VMEM((IDX_HEADS, tq, LANES), F32),
                        pltpu.VMEM((GROUP * tq, LANES), F32),
                        pltpu.VMEM((GROUP * tq, LANES), F32),
                        pltpu.VMEM((GROUP * tq, HEAD_DIM), F32)],
        compiler_params=_cparams(("arbitrary", "arbitrary")),
        name="dsa_prompt",
    )(q, iq, tail, kb16, vb16, ik2)


def _dsa_sample_kernel(ts, tk, past, n_sel, q_ref, iq_ref, tail_ref, pk_ref, pv_ref, pik2_ref,
                       nk_ref, nv_ref, nik2_ref, o_ref,
                       key_ref, nkey_ref, bias_ref, nbias_ref, lhs_ref, iwb_ref, kpad_ref, vpad_ref,
                       ikpad_ref, m_ref, l_ref, acc_ref):
    npb = past // tk
    _broadcast_index_weights(tail_ref, iwb_ref, ts)
    _stack_index_queries(iq_ref, lhs_ref, ts)
    kpad_ref[...] = jnp.zeros(kpad_ref.shape, BF16)
    vpad_ref[...] = jnp.zeros(vpad_ref.shape, BF16)
    ikpad_ref[...] = jnp.zeros(ikpad_ref.shape, BF16)
    kpad_ref[0:ts] = nk_ref[...]
    vpad_ref[0:ts] = nv_ref[...]
    ikpad_ref[0:ts] = nik2_ref[...]
    col = lax.broadcasted_iota(jnp.int32, (ts, LANES), 1)
    row = lax.broadcasted_iota(jnp.int32, (ts, LANES), 0)
    new_ok = (col < ts) & ((past + col) // CHUNK <= (past + row) // CHUNK)

    for kb in range(npb):
        score = _index_scores(lhs_ref, iwb_ref, pik2_ref[0, kb * tk:(kb + 1) * tk, :], ts)
        key_ref[kb] = _sort_key(score)
    nscore = _index_scores(lhs_ref, iwb_ref, ikpad_ref[...], ts)
    nkey_ref[...] = jnp.where(new_ok, _sort_key(nscore), INT_MIN)

    def count_ge(cand):
        c = jnp.where(nkey_ref[...] >= cand, 1.0, 0.0)
        for kb in range(npb):
            c = _count_block(key_ref[kb], cand, c)
        return _lane_total(c)
    thr1 = _kth_largest_key(count_ge, ts, n_sel)
    thr = _lane_tile(thr1, tk // LANES)
    for kb in range(npb):
        bias_ref[kb] = jnp.where(key_ref[kb] >= thr, 0.0, NEG)
    nbias_ref[...] = jnp.where(nkey_ref[...] >= thr1, 0.0, NEG)

    for n in range(N_KV_HEADS):
        qn = _group_queries(q_ref, n)
        _flash_init(m_ref, l_ref, acc_ref)
        hs = slice(n * HEAD_DIM, (n + 1) * HEAD_DIM)
        for kb in range(npb):
            kt = pk_ref[0, kb * tk:(kb + 1) * tk, hs].astype(BF16)
            vt = pv_ref[0, kb * tk:(kb + 1) * tk, hs].astype(BF16)
            s = _nt_dot(qn, kt)
            b = bias_ref[kb]
            _flash_step([s[g * ts:(g + 1) * ts] + b for g in range(GROUP)], vt, ts, m_ref, l_ref, acc_ref)
        s = _nt_dot(qn, kpad_ref[:, hs])
        b = nbias_ref[...]
        _flash_step([s[g * ts:(g + 1) * ts] + b for g in range(GROUP)], vpad_ref[:, hs], ts,
                    m_ref, l_ref, acc_ref)
        _flash_out(n, ts, l_ref, acc_ref, o_ref)


def _dsa_sample_call(q, iq, tail, pk, pv, pik2, nk, nv, nik2, batch, ts, past, n_sel):
    tk = _pick(past, (512, 256, 128))
    row_spec = lambda w: pl.BlockSpec((ts, w), lambda b: (b, 0))
    past_spec = lambda w: pl.BlockSpec((1, past, w), lambda b: (b, 0, 0))
    return pl.pallas_call(
        functools.partial(_dsa_sample_kernel, ts, tk, past, n_sel),
        grid=(batch,),
        in_specs=[row_spec(Q_W), row_spec(IQ_W), row_spec(LANES),
                  past_spec(KV_W), past_spec(KV_W), past_spec(LANES),
                  row_spec(KV_W), row_spec(KV_W), row_spec(LANES)],
        out_specs=row_spec(Q_W),
        out_shape=jax.ShapeDtypeStruct((batch * ts, Q_W), BF16),
        scratch_shapes=[pltpu.VMEM((past // tk, ts, tk), jnp.int32),
                        pltpu.VMEM((ts, LANES), jnp.int32),
                        pltpu.VMEM((past // tk, ts, tk), F32),
                        pltpu.VMEM((ts, LANES), F32),
                        pltpu.VMEM((IDX_HEADS * ts, LANES), BF16),
                        pltpu.VMEM((IDX_HEADS, ts, LANES), F32),
                        pltpu.VMEM((LANES, KV_W), BF16),
                        pltpu.VMEM((LANES, KV_W), BF16),
                        pltpu.VMEM((LANES, LANES), BF16),
                        pltpu.VMEM((GROUP * ts, LANES), F32),
                        pltpu.VMEM((GROUP * ts, LANES), F32),
                        pltpu.VMEM((GROUP * ts, HEAD_DIM), F32)],
        compiler_params=_cparams(("arbitrary",)),
        name="dsa_sample",
    )(q, iq, tail, pk, pv, pik2, nk, nv, nik2)


def _fox_prompt_kernel(tq, tk, q_ref, k_ref, v_ref, cum_ref, o_ref, m_ref, l_ref, acc_ref):
    q0 = pl.multiple_of(pl.program_id(1) * tq, tq)
    nkb = (q0 + tq + tk - 1) // tk
    c0 = cum_ref[0, :, pl.ds(q0, LANES)][:, 0:1]
    row = lax.broadcasted_iota(jnp.int32, (tq, tk), 0)
    col = lax.broadcasted_iota(jnp.int32, (tq, tk), 1)
    for n in range(N_KV_HEADS):
        qn = _group_queries(q_ref, n)
        _flash_init(m_ref, l_ref, acc_ref)

        def body(kb, _):
            k0 = pl.multiple_of(kb * tk, tk)
            kt = k_ref[pl.ds(k0, tk), n * HEAD_DIM:(n + 1) * HEAD_DIM]
            vt = v_ref[pl.ds(k0, tk), n * HEAD_DIM:(n + 1) * HEAD_DIM]
            s = _nt_dot(qn, kt)
            brow = c0 - cum_ref[0, :, pl.ds(k0, tk)]
            causal = k0 + col <= q0 + row
            slabs = []
            for g in range(GROUP):
                h = n * GROUP + g
                slabs.append(jnp.where(causal, s[g * tq:(g + 1) * tq] + brow[h:h + 1, :], NEG))
            _flash_step(slabs, vt, tq, m_ref, l_ref, acc_ref)
            return 0
        lax.fori_loop(0, nkb, body, 0)
        _flash_out(n, tq, l_ref, acc_ref, o_ref)


def _fox_prompt_call(q, kb16, vb16, cum, batch, t):
    tq = _pick(t, (128,))
    tk = _pick(t, (512, 256, 128))
    nq = t // tq
    row_spec = lambda w: pl.BlockSpec((tq, w), lambda b, i: (b * nq + i, 0))
    full_spec = lambda w: pl.BlockSpec((t, w), lambda b, i: (b, 0))
    return pl.pallas_call(
        functools.partial(_fox_prompt_kernel, tq, tk),
        grid=(batch, nq),
        in_specs=[row_spec(Q_W), full_spec(KV_W), full_spec(KV_W),
                  pl.BlockSpec((1, N_HEADS, t), lambda b, i: (b, 0, 0))],
        out_specs=row_spec(Q_W),
        out_shape=jax.ShapeDtypeStruct((batch * t, Q_W), BF16),
        scratch_shapes=[pltpu.VMEM((GROUP * tq, LANES), F32),
                        pltpu.VMEM((GROUP * tq, LANES), F32),
                        pltpu.VMEM((GROUP * tq, HEAD_DIM), F32)],
        compiler_params=_cparams(("arbitrary", "arbitrary")),
        name="fox_prompt",
    )(q, kb16, vb16, cum)


def _fox_sample_kernel(ts, tk, past, q_ref, pk_ref, pv_ref, nk_ref, nv_ref, cum_ref, o_ref,
                       kpad_ref, vpad_ref, m_ref, l_ref, acc_ref):
    npb = past // tk
    kpad_ref[...] = jnp.zeros(kpad_ref.shape, BF16)
    vpad_ref[...] = jnp.zeros(vpad_ref.shape, BF16)
    kpad_ref[0:ts] = nk_ref[...]
    vpad_ref[0:ts] = nv_ref[...]
    c0 = cum_ref[0, :, past - LANES:past][:, LANES - 1:LANES]
    col = lax.broadcasted_iota(jnp.int32, (ts, LANES), 1)
    row = lax.broadcasted_iota(jnp.int32, (ts, LANES), 0)
    new_ok = col <= row
    for n in range(N_KV_HEADS):
        qn = _group_queries(q_ref, n)
        _flash_init(m_ref, l_ref, acc_ref)
        hs = slice(n * HEAD_DIM, (n + 1) * HEAD_DIM)
        for kb in range(npb + 1):
            if kb < npb:
                kt = pk_ref[0, kb * tk:(kb + 1) * tk, hs].astype(BF16)
                vt = pv_ref[0, kb * tk:(kb + 1) * tk, hs].astype(BF16)
                brow = c0 - cum_ref[0, :, kb * tk:(kb + 1) * tk]
            else:
                kt, vt = kpad_ref[:, hs], vpad_ref[:, hs]
                brow = c0 - cum_ref[0, :, past:past + LANES]
            s = _nt_dot(qn, kt)
            slabs = []
            for g in range(GROUP):
                h = n * GROUP + g
                sg = s[g * ts:(g + 1) * ts] + brow[h:h + 1, :]
                slabs.append(sg if kb < npb else jnp.where(new_ok, sg, NEG))
            _flash_step(slabs, vt, ts, m_ref, l_ref, acc_ref)
        _flash_out(n, ts, l_ref, acc_ref, o_ref)


def _fox_sample_call(q, pk, pv, nk, nv, cum, batch, ts, past):
    tk = _pick(past, (512, 256, 128))
    lpad = cum.shape[2]
    row_spec = lambda w: pl.BlockSpec((ts, w), lambda b: (b, 0))
    past_spec = lambda w: pl.BlockSpec((1, past, w), lambda b: (b, 0, 0))
    return pl.pallas_call(
        functools.partial(_fox_sample_kernel, ts, tk, past),
        grid=(batch,),
        in_specs=[row_spec(Q_W), past_spec(KV_W), past_spec(KV_W), row_spec(KV_W), row_spec(KV_W),
                  pl.BlockSpec((1, N_HEADS, lpad), lambda b: (b, 0, 0))],
        out_specs=row_spec(Q_W),
        out_shape=jax.ShapeDtypeStruct((batch * ts, Q_W), BF16),
        scratch_shapes=[pltpu.VMEM((LANES, KV_W), BF16),
                        pltpu.VMEM((LANES, KV_W), BF16),
                        pltpu.VMEM((GROUP * ts, LANES), F32),
                        pltpu.VMEM((GROUP * ts, LANES), F32),
                        pltpu.VMEM((GROUP * ts, HEAD_DIM), F32)],
        compiler_params=_cparams(("arbitrary",)),
        name="fox_sample",
    )(q, pk, pv, nk, nv, cum)


HALO = 16


def _ffn_up_kernel(tm, bpb, h_ref, halo_ref, wg_ref, wv_ref, cwg_ref, cwv_ref, cbg_ref, cbv_ref,
                   a_ref, sg_ref, sv_ref, hbuf_ref, ug_ref, uv_ref):
    i, j = pl.program_id(0), pl.program_id(1)

    @pl.when(j == 0)
    def _():
        first = (i % bpb) == 0
        hbuf_ref[0:HALO] = jnp.where(first, jnp.zeros_like(halo_ref[...]), halo_ref[...])
        hbuf_ref[HALO:] = h_ref[...]

    hb = hbuf_ref[...]
    ug_ref[...] = jnp.dot(hb, wg_ref[0].astype(BF16), preferred_element_type=F32)
    uv_ref[...] = jnp.dot(hb, wv_ref[0].astype(BF16), preferred_element_type=F32)

    def conv(u_ref, cw_ref, cb_ref):
        z = cb_ref[0]
        for t in range(CONV_W):
            z = z + u_ref[HALO - (CONV_W - 1) + t:HALO - (CONV_W - 1) + t + tm] * cw_ref[0, t:t + 1, :]
        return z
    zg = conv(ug_ref, cwg_ref, cbg_ref)
    zv = conv(uv_ref, cwv_ref, cbv_ref)
    a_ref[...] = (zg * jax.nn.sigmoid(zg) * zv).astype(a_ref.dtype)
    sg_ref[0] = ug_ref[HALO + tm - (CONV_W - 1):HALO + tm]
    sv_ref[0] = uv_ref[HALO + tm - (CONV_W - 1):HALO + tm]


def _ffn_up_call(h, w_up, conv_w, conv_b3, l, batch, t):
    m, d = h.shape
    dff = w_up.shape[2] // 2
    tm = _pick(t, (1024, 512, 256, 128))
    tn = _pick(dff, (512, 256, 128))
    ncb = dff // tn
    bpb = t // tm
    hpb = tm // HALO
    st = jax.ShapeDtypeStruct((batch, CONV_W - 1, dff), F32)
    st_spec = pl.BlockSpec((1, CONV_W - 1, tn), lambda i, j: (i // bpb, 0, j))
    return pl.pallas_call(
        functools.partial(_ffn_up_kernel, tm, bpb),
        grid=(m // tm, ncb),
        in_specs=[pl.BlockSpec((tm, d), lambda i, j: (i, 0)),
                  pl.BlockSpec((HALO, d), lambda i, j: (jnp.maximum(i * hpb - 1, 0), 0)),
                  pl.BlockSpec((1, d, tn), lambda i, j: (l, 0, j)),
                  pl.BlockSpec((1, d, tn), lambda i, j: (l, 0, ncb + j)),
                  pl.BlockSpec((1, CONV_W, tn), lambda i, j: (l, 0, j)),
                  pl.BlockSpec((1, CONV_W, tn), lambda i, j: (l, 0, ncb + j)),
                  pl.BlockSpec((1, 1, tn), lambda i, j: (l, 0, j)),
                  pl.BlockSpec((1, 1, tn), lambda i, j: (l, 0, ncb + j))],
        out_specs=[pl.BlockSpec((tm, tn), lambda i, j: (i, j)), st_spec, st_spec],
        out_shape=[jax.ShapeDtypeStruct((m, dff), BF16), st, st],
        scratch_shapes=[pltpu.VMEM((HALO + tm, d), BF16),
                        pltpu.VMEM((HALO + tm, tn), F32),
                        pltpu.VMEM((HALO + tm, tn), F32)],
        compiler_params=_cparams(("arbitrary", "arbitrary")),
        name="ffn_up_conv_gate",
    )(h, h, w_up, w_up, conv_w, conv_w, conv_b3, conv_b3)


def _conv_gate_kernel(u0g, u1g, u2g, u0v, u1v, u2v, cwg_ref, cwv_ref, cbg_ref, cbv_ref, a_ref):
    zg = cbg_ref[0] + u0g[...] * cwg_ref[0, 0:1, :] + u1g[...] * cwg_ref[0, 1:2, :] + u2g[...] * cwg_ref[0, 2:3, :]
    zv = cbv_ref[0] + u0v[...] * cwv_ref[0, 0:1, :] + u1v[...] * cwv_ref[0, 1:2, :] + u2v[...] * cwv_ref[0, 2:3, :]
    a_ref[...] = (zg * jax.nn.sigmoid(zg) * zv).astype(a_ref.dtype)


def _conv_gate_call(shifted, conv_w, conv_b3, l):
    m, n2 = shifted[0].shape
    dff = n2 // 2
    tn = _pick(dff, (512, 256, 128))
    ncb = dff // tn
    g_spec = pl.BlockSpec((m, tn), lambda j: (0, j))
    v_spec = pl.BlockSpec((m, tn), lambda j: (0, ncb + j))
    return pl.pallas_call(
        _conv_gate_kernel,
        grid=(ncb,),
        in_specs=[g_spec, g_spec, g_spec, v_spec, v_spec, v_spec,
                  pl.BlockSpec((1, CONV_W, tn), lambda j: (l, 0, j)),
                  pl.BlockSpec((1, CONV_W, tn), lambda j: (l, 0, ncb + j)),
                  pl.BlockSpec((1, 1, tn), lambda j: (l, 0, j)),
                  pl.BlockSpec((1, 1, tn), lambda j: (l, 0, ncb + j))],
        out_specs=pl.BlockSpec((m, tn), lambda j: (0, j)),
        out_shape=jax.ShapeDtypeStruct((m, dff), BF16),
        compiler_params=_cparams(("arbitrary",)),
        name="conv_gate",
    )(*shifted, *shifted, conv_w, conv_w, conv_b3, conv_b3)


def _trunk(x, mods, batch, t, past, caches, w):
    m, d = x.shape
    depth = w['norm_mix_g'].shape[0]
    tm = _pick(m, (1024, 512, 256, 128))
    length = past + t
    n_sel = min(TOPK_MAX, length // 4)
    pos = (past + jnp.arange(t, dtype=jnp.int32)).astype(F32)
    if t % tm == 0:
        rep_rows = lambda tab: tab
        tab_spec = lambda: pl.BlockSpec((tm, 3 * LANES), lambda i, j, k: (i % (t // tm), 0))
    else:
        rep_rows = lambda tab: jnp.tile(tab, (m // t, 1))
        tab_spec = lambda: pl.BlockSpec((tm, 3 * LANES), lambda i, j, k: (i, 0))
    scale = HEAD_DIM ** -0.5

    def tail_lanes(a, bm, bp):
        lane = np.arange(LANES)
        is_iw = jnp.asarray((lane >= IDX_DIM) & (lane < IDX_DIM + IDX_HEADS))[None, :]
        return jnp.where(is_iw, IDX_HEADS ** -0.5 * IDX_DIM ** -0.5, a), bm, bp

    def tail_weights(w_in, c0, ncols):
        return jnp.pad(w_in[:, :, c0:c0 + ncols], ((0, 0), (0, 0), (0, LANES - ncols)))

    tab_q = rep_rows(_rope_table(pos, ROT_DIM, HEAD_DIM, scale))
    tab_k = rep_rows(_rope_table(pos, ROT_DIM, HEAD_DIM))
    tab_iq = rep_rows(_rope_table(pos, IDX_ROT_DIM, IDX_DIM))
    tab_tail = rep_rows(_rope_table(pos, IDX_ROT_DIM, LANES, const_lanes=tail_lanes))

    st = dict(ka=[], va=[], ika=[], kb=[], vb=[], lfb=[], us=[])
    for l in range(depth):
        sh_m, sc_m, g_m, sh_f, sc_f, g_f = mods[l]
        jmix = l // N_MIXERS
        h = _norm_call(x, w['norm_mix_g3'], l, sc_m, sh_m, BF16)
        lin = functools.partial(_linear_call, a=h, tm=tm, tk=d)
        if l % N_MIXERS == 0:
            w_in = w['dsa_w_in']
            tn = 512
            (q,) = lin("dsa_q", functools.partial(_ep_rope, ROT_DIM // 2), w3=w_in, l=jmix, col0=0, ncols=Q_W,
                       tn=tn, extras=[(tab_q, tab_spec())], out_dtypes=[BF16])
            k32, k16 = lin("dsa_k", functools.partial(_ep_rope, ROT_DIM // 2), w3=w_in, l=jmix, col0=Q_W,
                           ncols=KV_W, tn=tn, extras=[(tab_k, tab_spec())], out_dtypes=[F32, BF16])
            v32, v16 = lin("dsa_v", _ep_plain, w3=w_in, l=jmix, col0=Q_W + KV_W, ncols=KV_W, tn=tn,
                           extras=[], out_dtypes=[F32, BF16])
            (iq,) = lin("dsa_iq", functools.partial(_ep_rope, IDX_ROT_DIM // 2), w3=w_in, l=jmix,
                        col0=Q_W + 2 * KV_W, ncols=IQ_W, tn=tn, extras=[(tab_iq, tab_spec())], out_dtypes=[BF16])
            w_tail = tail_weights(w_in, Q_W + 2 * KV_W + IQ_W, IDX_DIM + IDX_HEADS)
            (tail,) = lin("dsa_tail", functools.partial(_ep_rope, IDX_ROT_DIM // 2), w3=w_tail, l=jmix,
                          col0=0, ncols=LANES, tn=LANES, extras=[(tab_tail, tab_spec())], out_dtypes=[F32])
            ik = tail[:, :IDX_DIM]
            ik2 = jnp.concatenate([ik, ik], axis=1).astype(BF16)
            if caches is None:
                attn = _dsa_prompt_call(q, iq, tail, k16, v16, ik2, batch, t, n_sel)
            else:
                pk = caches['dsa_k'][jmix].reshape(batch, past, KV_W)
                pv = caches['dsa_v'][jmix].reshape(batch, past, KV_W)
                pik = caches['dsa_ik'][jmix]
                pik2 = jnp.concatenate([pik, pik], axis=-1).astype(BF16)
                attn = _dsa_sample_call(q, iq, tail, pk, pv, pik2, k16, v16, ik2, batch, t, past, n_sel)
            st['ka'].append(k32.reshape(batch, t, N_KV_HEADS, HEAD_DIM))
            st['va'].append(v32.reshape(batch, t, N_KV_HEADS, HEAD_DIM))
            st['ika'].append(ik.reshape(batch, t, IDX_DIM))
            w_o = w['dsa_w_o']
        else:
            w_in = w['fox_w_in']
            tn = 512
            (q,) = lin("fox_q", functools.partial(_ep_scale, scale), w3=w_in, l=jmix, col0=0, ncols=Q_W, tn=tn,
                       extras=[], out_dtypes=[BF16])
            k32, k16 = lin("fox_k", _ep_plain, w3=w_in, l=jmix, col0=Q_W, ncols=KV_W, tn=tn, extras=[],
                           out_dtypes=[F32, BF16])
            v32, v16 = lin("fox_v", _ep_plain, w3=w_in, l=jmix, col0=Q_W + KV_W, ncols=KV_W, tn=tn, extras=[],
                           out_dtypes=[F32, BF16])
            bf = jnp.pad(w['fox_b_f'][jmix], (0, LANES - N_HEADS)).reshape(1, LANES)
            w_tail = tail_weights(w_in, Q_W + 2 * KV_W, N_HEADS)
            (lf_tail,) = lin("fox_logf", _ep_logsig, w3=w_tail, l=jmix, col0=0, ncols=LANES, tn=LANES,
                             extras=[(bf, pl.BlockSpec((1, LANES), lambda i, j, k: (0, 0)))], out_dtypes=[F32])
            logf = lf_tail[:, :N_HEADS].reshape(batch, t, N_HEADS)
            if caches is None:
                lf_t = jnp.swapaxes(logf, 1, 2).reshape(batch * N_HEADS, t)
                cum = _cumsum_call(lf_t).reshape(batch, N_HEADS, t)
                attn = _fox_prompt_call(q, k16, v16, cum, batch, t)
            else:
                lf_all = jnp.concatenate([caches['fox_lf'][jmix], logf], axis=1)
                lpad = past + LANES
                lf_t = jnp.pad(jnp.swapaxes(lf_all, 1, 2), ((0, 0), (0, 0), (0, lpad - length)))
                cum = _cumsum_call(lf_t.reshape(batch * N_HEADS, lpad)).reshape(batch, N_HEADS, lpad)
                pk = caches['fox_k'][jmix].reshape(batch, past, KV_W)
                pv = caches['fox_v'][jmix].reshape(batch, past, KV_W)
                attn = _fox_sample_call(q, pk, pv, k16, v16, cum, batch, t, past)
            st['kb'].append(k32.reshape(batch, t, N_KV_HEADS, HEAD_DIM))
            st['vb'].append(v32.reshape(batch, t, N_KV_HEADS, HEAD_DIM))
            st['lfb'].append(logf)
            w_o = w['fox_w_o']

        tn_o = _pick(d, (512,))
        res_extras = lambda xx, gg: [(xx, pl.BlockSpec((tm, tn_o), lambda i, j, k: (i, j))),
                                     (gg, _mod_spec(gg, m, tm, tn_o, lambda i, j, k: j))]
        (x,) = _linear_call("attn_out", _ep_residual, attn, w_o, jmix, 0, d, tn_o, tm, Q_W,
                            res_extras(x, g_m), [F32])

        h = _norm_call(x, w['norm_ffn_g3'], l, sc_f, sh_f, BF16)
        dff = w['ffn_w_down'].shape[1]
        if caches is None:
            a, sg, sv = _ffn_up_call(h, w['ffn_w_up'], w['ffn_conv_w'], w['ffn_conv_b3'], l, batch, t)
            st['us'].append(jnp.concatenate([sg, sv], axis=-1))
        else:
            (u,) = _linear_call("ffn_up", _ep_plain, h, w['ffn_w_up'], l, 0, 2 * dff, 512, tm, d, [], [F32])
            u_all = jnp.concatenate([caches['ffn_u'][l], u.reshape(batch, t, 2 * dff)], axis=1)
            shifted = [u_all[:, s:s + t].reshape(m, 2 * dff) for s in range(CONV_W)]
            a = _conv_gate_call(shifted, w['ffn_conv_w'], w['ffn_conv_b3'], l)
            st['us'].append(u_all[:, t:])
        tk_d = _pick(dff, (1408, 1024, 512))
        (x,) = _linear_call("ffn_down", _ep_residual, a, w['ffn_w_down'], l, 0, d, tn_o, tm, tk_d,
                            res_extras(x, g_f), [F32])

    y = _norm_call(x, w['final_g3'], 0, None, None, F32)
    return y, st


def kernel(x_prompt, x_sample, c_prompt, c_sample, cache_dsa_k, cache_dsa_v, cache_dsa_idx_k, cache_fox_k,
           cache_fox_v, cache_fox_logf, state_ffn_conv, norm_mix_g, norm_ffn_g, w_ada, b_ada, dsa_w_in, dsa_w_o,
           fox_w_in, fox_b_f, fox_w_o, ffn_w_up, ffn_conv_w, ffn_conv_b, ffn_w_down, final_g):
    b, t, d = x_prompt.shape
    bs, ts, _ = x_sample.shape
    past = cache_dsa_k.shape[2]
    depth = norm_mix_g.shape[0]
    w = dict(norm_mix_g=norm_mix_g, norm_mix_g3=norm_mix_g.reshape(depth, 1, d),
             norm_ffn_g3=norm_ffn_g.reshape(depth, 1, d), final_g3=final_g.reshape(1, 1, d),
             dsa_w_in=dsa_w_in, dsa_w_o=dsa_w_o, fox_w_in=fox_w_in, fox_b_f=fox_b_f, fox_w_o=fox_w_o,
             ffn_w_up=ffn_w_up, ffn_conv_w=ffn_conv_w, ffn_conv_b3=ffn_conv_b.reshape(depth, 1, -1),
             ffn_w_down=ffn_w_down)

    nc = b + bs
    nc_pad = -(-nc // 8) * 8
    c_all = jnp.pad(jnp.concatenate([c_prompt, c_sample], axis=0), ((0, nc_pad - nc), (0, 0)))
    mod = _ada_call(c_all, w_ada, b_ada)
    mods_p, mods_s = [], []
    for l in range(depth):
        parts = [mod[l, :, i * d:(i + 1) * d] for i in range(6)]
        mods_p.append([p[:b].reshape(b, 1, d) for p in parts])
        mods_s.append([jnp.repeat(p[b:nc], ts, axis=0).reshape(1, bs * ts, d) for p in parts])

    y_p, sp = _trunk(x_prompt.reshape(b * t, d), mods_p, b, t, 0, None, w)
    caches = dict(dsa_k=cache_dsa_k, dsa_v=cache_dsa_v, dsa_ik=cache_dsa_idx_k, fox_k=cache_fox_k,
                  fox_v=cache_fox_v, fox_lf=cache_fox_logf, ffn_u=state_ffn_conv)
    y_s, ss = _trunk(x_sample.reshape(bs * ts, d), mods_s, bs, ts, past, caches, w)

    names = ('ka', 'va', 'ika', 'kb', 'vb', 'lfb', 'us')
    return ((y_p.reshape(b, t, d), y_s.reshape(bs, ts, d))
            + tuple(jnp.stack(sp[n]) for n in names) + tuple(jnp.stack(ss[n]) for n in names))
```

```python
import functools

import jax
import jax.numpy as jnp
import numpy as np
from jax import lax
from jax.experimental import pallas as pl
from jax.experimental.pallas import tpu as pltpu

F32 = jnp.float32
BF16 = jnp.bfloat16

CHUNK = 64
N_MIXERS = 2
N_HEADS = 16
N_KV_HEADS = 4
GROUP = N_HEADS // N_KV_HEADS
HEAD_DIM = 128
ROT_DIM = HEAD_DIM // 4
IDX_HEADS = 16
IDX_DIM = 64
IDX_ROT_DIM = IDX_DIM // 4
TOPK_MAX = 256
ROPE_THETA = 500000.0
CONV_W = 3
EPS = 1e-6
Q_W = N_HEADS * HEAD_DIM
KV_W = N_KV_HEADS * HEAD_DIM
IQ_W = IDX_HEADS * IDX_DIM

LANES = 128
NEG = -1e30
LOG2E = 1.4426950408889634
INT_MIN = -(2 ** 31)
VMEM_LIMIT = 56 * 2 ** 20


def _cparams(sem):
    return pltpu.CompilerParams(dimension_semantics=sem, vmem_limit_bytes=VMEM_LIMIT)


def _pick(n, cands):
    for c in cands:
        if n % c == 0:
            return c
    return n


def _ada_kernel(c_ref, w_ref, b_ref, o_ref):
    c = c_ref[...]
    cs = (c * jax.nn.sigmoid(c)).astype(BF16)
    o_ref[0] = jnp.dot(cs, w_ref[0].astype(BF16), preferred_element_type=F32) + b_ref[0]


def _ada_call(c_all, w_ada, b_ada):
    depth, d, n6 = w_ada.shape
    bc = c_all.shape[0]
    tn = _pick(n6, (1024, 512, 256, 128))
    return pl.pallas_call(
        _ada_kernel,
        grid=(depth, n6 // tn),
        in_specs=[pl.BlockSpec((bc, d), lambda l, j: (0, 0)),
                  pl.BlockSpec((1, d, tn), lambda l, j: (l, 0, j)),
                  pl.BlockSpec((1, 1, tn), lambda l, j: (l, 0, j))],
        out_specs=pl.BlockSpec((1, bc, tn), lambda l, j: (l, 0, j)),
        out_shape=jax.ShapeDtypeStruct((depth, bc, n6), F32),
        compiler_params=_cparams(("arbitrary", "arbitrary")),
        name="ada_mod",
    )(c_all, w_ada, b_ada.reshape(depth, 1, n6))


def _mod_spec(arr3, m, tm, tn, col_of):
    g, r, _ = arr3.shape
    if r == 1:
        bpg = (m // g) // tm
        return pl.BlockSpec((1, 1, tn), lambda *ix: (ix[0] // bpg, 0, col_of(*ix)))
    assert g == 1 and r == m
    return pl.BlockSpec((1, tm, tn), lambda *ix: (0, ix[0], col_of(*ix)))


def _norm_kernel(modulate, x_ref, g_ref, *refs):
    x = x_ref[...]
    y = x * lax.rsqrt(jnp.mean(x * x, axis=-1, keepdims=True) + EPS) * g_ref[0]
    if modulate:
        sc_ref, sh_ref, o_ref = refs
        y = y * (1.0 + sc_ref[0]) + sh_ref[0]
    else:
        (o_ref,) = refs
    o_ref[...] = y.astype(o_ref.dtype)


def _norm_call(x, g3, l, sc3, sh3, out_dtype):
    m, d = x.shape
    tm = _pick(m, (512, 256, 128))
    in_specs = [pl.BlockSpec((tm, d), lambda i: (i, 0)),
                pl.BlockSpec((1, 1, d), lambda i: (l, 0, 0))]
    args = [x, g3]
    if sc3 is not None:
        in_specs += [_mod_spec(sc3, m, tm, d, lambda i: 0), _mod_spec(sh3, m, tm, d, lambda i: 0)]
        args += [sc3, sh3]
    return pl.pallas_call(
        functools.partial(_norm_kernel, sc3 is not None),
        grid=(m // tm,),
        in_specs=in_specs,
        out_specs=pl.BlockSpec((tm, d), lambda i: (i, 0)),
        out_shape=jax.ShapeDtypeStruct((m, d), out_dtype),
        compiler_params=_cparams(("arbitrary",)),
        name="rms_norm",
    )(*args)


def _mm_kernel(epilogue, nk, n_extra, a_ref, w_ref, *refs):
    extra = refs[:n_extra]
    if nk == 1:
        outs = refs[n_extra:]
        epilogue(jnp.dot(a_ref[...], w_ref[0].astype(BF16), preferred_element_type=F32), extra, outs)
        return
    outs, acc_ref = refs[n_extra:-1], refs[-1]
    k = pl.program_id(2)
    part = jnp.dot(a_ref[...], w_ref[0].astype(BF16), preferred_element_type=F32)

    @pl.when(k == 0)
    def _():
        acc_ref[...] = part

    @pl.when(k > 0)
    def _():
        acc_ref[...] += part

    @pl.when(k == nk - 1)
    def _():
        epilogue(acc_ref[...], extra, outs)


def _linear_call(name, epilogue, a, w3, l, col0, ncols, tn, tm, tk, extras, out_dtypes, head_flat=0):
    m, kdim = a.shape
    assert col0 % tn == 0 and m % tm == 0 and kdim % tk == 0
    cb0, ncb, nk = col0 // tn, -(-ncols // tn), kdim // tk
    in_specs = [pl.BlockSpec((tm, tk), lambda i, j, k: (i, k)),
                pl.BlockSpec((1, tk, tn), lambda i, j, k: (l, k, cb0 + j))]
    in_specs += [s for _, s in extras]
    outs = [jax.ShapeDtypeStruct((m, ncb * tn), dt) for dt in out_dtypes]
    out_specs = [pl.BlockSpec((tm, tn), lambda i, j, k: (i, j)) for _ in out_dtypes]
    assert head_flat == 0 or ncb == 1
    for o in range(head_flat):
        heads = tn // HEAD_DIM
        outs[o] = jax.ShapeDtypeStruct((m * heads, HEAD_DIM), out_dtypes[o])
        out_specs[o] = pl.BlockSpec((tm * heads, HEAD_DIM), lambda i, j, k: (i, 0))
    scratch = [pltpu.VMEM((tm, tn), F32)] if nk > 1 else []
    return pl.pallas_call(
        functools.partial(_mm_kernel, epilogue, nk, len(extras)),
        grid=(m // tm, ncb, nk),
        in_specs=in_specs,
        out_specs=out_specs,
        out_shape=outs,
        scratch_shapes=scratch,
        compiler_params=_cparams(("arbitrary", "arbitrary", "arbitrary")),
        name=name,
    )(a, w3, *[x for x, _ in extras])


def _lane_tile(x, reps):
    return x if reps == 1 else jnp.concatenate([x] * reps, axis=1)


def _ep_rope(shift, acc, extra, outs):
    (tab_ref,) = extra
    n = acc.shape[1]
    reps = n // LANES
    a = _lane_tile(tab_ref[:, 0:LANES], reps)
    bm = _lane_tile(tab_ref[:, LANES:2 * LANES], reps)
    bp = _lane_tile(tab_ref[:, 2 * LANES:3 * LANES], reps)
    y = acc * a + pltpu.roll(acc, n - shift, 1) * bm + pltpu.roll(acc, shift, 1) * bp
    for o in outs:
        _store_tile(o, y)


def _store_tile(o_ref, y):
    rows, n = y.shape
    if o_ref.shape == (rows, n):
        o_ref[...] = y.astype(o_ref.dtype)
        return
    heads = n // HEAD_DIM
    assert o_ref.shape == (rows * heads, HEAD_DIM)
    for h in range(heads):
        o_ref[pl.ds(h, rows, stride=heads), :] = y[:, h * HEAD_DIM:(h + 1) * HEAD_DIM].astype(o_ref.dtype)


def _ep_scale(scale, acc, extra, outs):
    for o in outs:
        _store_tile(o, acc * scale)


def _ep_plain(acc, extra, outs):
    for o in outs:
        _store_tile(o, acc)


def _ep_logsig(acc, extra, outs):
    (b_ref,) = extra
    x = acc + b_ref[...]
    y = jnp.minimum(x, 0.0) - jnp.log1p(jnp.exp(-jnp.abs(x)))
    outs[0][...] = y


def _ep_residual(acc, extra, outs):
    x_ref, g_ref = extra
    outs[0][...] = x_ref[...] + g_ref[0] * acc


def _rope_table(pos, rot_dim, period, scale=1.0, const_lanes=None):
    half = rot_dim // 2
    inv = ROPE_THETA ** (-jnp.arange(half, dtype=F32) * 2.0 / rot_dim)
    ang = pos[:, None] * inv[None, :]
    cos, sin = jnp.cos(ang), jnp.sin(ang)
    lane = np.arange(LANES) % period
    idx = lane % half
    lo = jnp.asarray(lane < half)[None, :]
    hi = jnp.asarray((lane >= half) & (lane < rot_dim))[None, :]
    a = jnp.where(lo | hi, cos[:, idx], 1.0)
    bm = jnp.where(lo, -sin[:, idx], 0.0)
    bp = jnp.where(hi, sin[:, idx], 0.0)
    if const_lanes is not None:
        a, bm, bp = const_lanes(a, bm, bp)
    return jnp.concatenate([a, bm, bp], axis=1) * scale


def _split3_dot(x, u):
    hi = x.astype(BF16)
    r1 = x - hi.astype(F32)
    mid = r1.astype(BF16)
    lo = (r1 - mid.astype(F32)).astype(BF16)
    d = lambda p: jnp.dot(p, u, preferred_element_type=F32)
    return d(hi) + (d(mid) + d(lo))


def _cumsum_kernel(nb, x_ref, o_ref):
    r = lax.broadcasted_iota(jnp.int32, (LANES, LANES), 0)
    c = lax.broadcasted_iota(jnp.int32, (LANES, LANES), 1)
    u = jnp.where(r <= c, 1.0, 0.0).astype(BF16)
    carry = jnp.zeros((x_ref.shape[0], 1), F32)
    for b in range(nb):
        y = _split3_dot(x_ref[:, b * LANES:(b + 1) * LANES], u) + carry
        o_ref[:, b * LANES:(b + 1) * LANES] = y
        carry = y[:, LANES - 1:LANES]


def _cumsum_call(x):
    r, length = x.shape
    tr = _pick(r, (256, 128, 64, 32, 16, 8))
    return pl.pallas_call(
        functools.partial(_cumsum_kernel, length // LANES),
        grid=(r // tr,),
        in_specs=[pl.BlockSpec((tr, length), lambda i: (i, 0))],
        out_specs=pl.BlockSpec((tr, length), lambda i: (i, 0)),
        out_shape=jax.ShapeDtypeStruct((r, length), F32),
        compiler_params=_cparams(("arbitrary",)),
        name="cumsum",
    )(x)


def _nt_dot(a, b):
    return lax.dot_general(a, b, (((1,), (1,)), ((), ())), preferred_element_type=F32)


def _flash_init(m_ref, l_ref, acc_ref):
    m_ref[...] = jnp.full(m_ref.shape, NEG, F32)
    l_ref[...] = jnp.zeros(l_ref.shape, F32)
    acc_ref[...] = jnp.zeros(acc_ref.shape, F32)


def _flash_step(slabs, vt, rows, m_ref, l_ref, acc_ref):
    tk = slabs[0].shape[1]
    ps, alphas = [], []
    for g, s in enumerate(slabs):
        sl = slice(g * rows, (g + 1) * rows)
        m_prev = m_ref[sl]
        m_new = jnp.maximum(m_prev, jnp.max(s, axis=1, keepdims=True))
        alpha = jnp.exp2(m_prev - m_new)
        p = jnp.exp2(s - _lane_tile(m_new, tk // LANES))
        l_ref[sl] = alpha * l_ref[sl] + jnp.sum(p, axis=1, keepdims=True)
        m_ref[sl] = m_new
        ps.append(p.astype(BF16))
        alphas.append(alpha)
    pv = jnp.dot(jnp.concatenate(ps, axis=0), vt, preferred_element_type=F32)
    for g, alpha in enumerate(alphas):
        sl = slice(g * rows, (g + 1) * rows)
        acc_ref[sl] = acc_ref[sl] * alpha + pv[sl]


def _flash_out(n, rows, l_ref, acc_ref, o_ref):
    for g in range(GROUP):
        sl = slice(g * rows, (g + 1) * rows)
        h = n * GROUP + g
        o_ref[:, h * HEAD_DIM:(h + 1) * HEAD_DIM] = (acc_ref[sl] / l_ref[sl]).astype(o_ref.dtype)


def _group_queries(q_ref, n):
    return jnp.concatenate(
        [q_ref[:, (n * GROUP + g) * HEAD_DIM:(n * GROUP + g + 1) * HEAD_DIM] for g in range(GROUP)], axis=0)


def _sort_key(score):
    bits = pltpu.bitcast(score, jnp.int32)
    return bits ^ ((bits >> 31) & jnp.int32(0x7FFFFFFF))


def _stack_index_queries(iq_ref, lhs_ref, rows):
    lane = lax.broadcasted_iota(jnp.int32, (rows, LANES), 1)
    zero = jnp.zeros((rows, LANES), BF16)
    for p in range(IDX_HEADS // 2):
        pair = iq_ref[:, p * LANES:(p + 1) * LANES]
        lhs_ref[(2 * p) * rows:(2 * p + 1) * rows] = jnp.where(lane < IDX_DIM, pair, zero)
        lhs_ref[(2 * p + 1) * rows:(2 * p + 2) * rows] = jnp.where(lane >= IDX_DIM, pair, zero)


def _index_scores(lhs_ref, iwb_ref, ik2_blk, rows):
    tk = ik2_blk.shape[0]
    s_all = _nt_dot(lhs_ref[...], ik2_blk)
    acc = jnp.zeros((rows, tk), F32)
    for h in range(IDX_HEADS):
        acc = acc + jnp.maximum(s_all[h * rows:(h + 1) * rows], 0.0) * _lane_tile(iwb_ref[h], tk // LANES)
    return acc


def _broadcast_index_weights(tail_ref, iwb_ref, rows):
    tail = tail_ref[...]
    for h in range(IDX_HEADS):
        iwb_ref[h] = jnp.broadcast_to(tail[:, IDX_DIM + h:IDX_DIM + h + 1], (rows, LANES))


def _kth_largest_key(count_ge, rows, n_sel):
    def bit_body(it, t):
        cand = t + jnp.left_shift(jnp.int32(1), 31 - it)
        cnt = count_ge(cand)
        return jnp.where(cnt >= float(n_sel), cand, t)
    t = lax.fori_loop(0, 32, bit_body, jnp.full((rows, LANES), INT_MIN, jnp.int32))
    return jnp.maximum(t, INT_MIN + 1)


def _count_block(key, cand, c):
    for cc in range(key.shape[1] // LANES):
        c = c + jnp.where(key[:, cc * LANES:(cc + 1) * LANES] >= cand, 1.0, 0.0)
    return c


def _lane_total(c):
    return jnp.broadcast_to(jnp.sum(c, axis=1, keepdims=True), c.shape)


def _dsa_prompt_kernel(tq, tk, n_sel, q_ref, iq_ref, tail_ref, k_ref, v_ref, ik2_ref, o_ref,
                       key_ref, bias_ref, lhs_ref, iwb_ref, qg_ref, m_ref, l_ref, acc_ref):
    q0 = pl.program_id(1) * tq
    nkb = (q0 + tq + tk - 1) // tk
    _broadcast_index_weights(tail_ref, iwb_ref, tq)
    _stack_index_queries(iq_ref, lhs_ref, tq)
    row = lax.broadcasted_iota(jnp.int32, (tq, tk), 0)
    col = lax.broadcasted_iota(jnp.int32, (tq, tk), 1)
    lim = ((q0 + row) // CHUNK + 1) * CHUNK

    def score_body(kb, _):
        k0 = pl.multiple_of(kb * tk, tk)
        score = _index_scores(lhs_ref, iwb_ref, ik2_ref[pl.ds(k0, tk), :], tq)
        key_ref[kb] = jnp.where(k0 + col < lim, _sort_key(score), INT_MIN)
        return 0
    lax.fori_loop(0, nkb, score_body, 0)

    def count_ge(cand):
        c = lax.fori_loop(0, nkb, lambda kb, c: _count_block(key_ref[kb], cand, c),
                          jnp.zeros((tq, LANES), F32))
        return _lane_total(c)
    thr = _lane_tile(_kth_largest_key(count_ge, tq, n_sel), tk // LANES)

    def bias_body(kb, _):
        bias_ref[kb] = jnp.where(key_ref[kb] >= thr, 0.0, NEG)
        return 0
    lax.fori_loop(0, nkb, bias_body, 0)

    for n in range(N_KV_HEADS):
        qg_ref[n] = _group_queries(q_ref, n)
    _flash_init(m_ref, l_ref, acc_ref)

    def body(kb, _):
        k0 = pl.multiple_of(kb * tk, tk)
        b = bias_ref[kb]
        for n in range(N_KV_HEADS):
            kt = k_ref[pl.ds(k0, tk), n * HEAD_DIM:(n + 1) * HEAD_DIM]
            vt = v_ref[pl.ds(k0, tk), n * HEAD_DIM:(n + 1) * HEAD_DIM]
            s = _nt_dot(qg_ref[n], kt)
            _flash_step([s[g * tq:(g + 1) * tq] + b for g in range(GROUP)], vt, tq,
                        m_ref.at[n], l_ref.at[n], acc_ref.at[n])
        return 0
    lax.fori_loop(0, nkb, body, 0)
    for n in range(N_KV_HEADS):
        _flash_out(n, tq, l_ref.at[n], acc_ref.at[n], o_ref)


def _flash_scratch(tq):
    return [pltpu.VMEM((N_KV_HEADS, GROUP * tq, HEAD_DIM), BF16),
            pltpu.VMEM((N_KV_HEADS, GROUP * tq, LANES), F32),
            pltpu.VMEM((N_KV_HEADS, GROUP * tq, LANES), F32),
            pltpu.VMEM((N_KV_HEADS, GROUP * tq, HEAD_DIM), F32)]


def _dsa_prompt_call(q, iq, tail, kb16, vb16, ik2, batch, t, n_sel):
    tq = _pick(t, (128,))
    tk = _pick(t, (512, 256, 128))
    nq = t // tq
    row_spec = lambda w: pl.BlockSpec((tq, w), lambda b, i: (b * nq + i, 0))
    full_spec = lambda w: pl.BlockSpec((t, w), lambda b, i: (b, 0))
    return pl.pallas_call(
        functools.partial(_dsa_prompt_kernel, tq, tk, n_sel),
        grid=(batch, nq),
        in_specs=[row_spec(Q_W), row_spec(IQ_W), row_spec(LANES),
                  full_spec(KV_W), full_spec(KV_W), full_spec(LANES)],
        out_specs=row_spec(Q_W),
        out_shape=jax.ShapeDtypeStruct((batch * t, Q_W), BF16),
        scratch_shapes=[pltpu.VMEM((t // tk, tq, tk), jnp.int32),
                        pltpu.VMEM((t // tk, tq, tk), F32),
                        pltpu.VMEM((IDX_HEADS * tq, LANES), BF16),
                        pltpu.VMEM((IDX_HEADS, tq, LANES), F32)] + _flash_scratch(tq),
        compiler_params=_cparams(("arbitrary", "arbitrary")),
        name="dsa_prompt",
    )(q, iq, tail, kb16, vb16, ik2)


def _sample_head_attention(n, ts, q_ref, kn, vn, knew, vnew, bias_past, bias_new, o_ref):
    qn = _group_queries(q_ref, n)
    s_past = _nt_dot(qn, kn)
    s_new = _nt_dot(qn, knew)
    sp = jnp.concatenate([s_past[g * ts:(g + 1) * ts] + bias_past(g) for g in range(GROUP)], axis=0)
    sn = jnp.concatenate([s_new[g * ts:(g + 1) * ts] + bias_new(g) for g in range(GROUP)], axis=0)
    m = jnp.maximum(jnp.max(sp, axis=1, keepdims=True), jnp.max(sn, axis=1, keepdims=True))
    pp = jnp.exp2(sp - m)
    pn = jnp.exp2(sn - m)
    l = jnp.sum(pp, axis=1, keepdims=True) + jnp.sum(pn, axis=1, keepdims=True)
    o = (jnp.dot(pp.astype(BF16), vn, preferred_element_type=F32)
         + jnp.dot(pn.astype(BF16), vnew, preferred_element_type=F32)) / l
    for g in range(GROUP):
        h = n * GROUP + g
        o_ref[:, h * HEAD_DIM:(h + 1) * HEAD_DIM] = o[g * ts:(g + 1) * ts].astype(o_ref.dtype)


def _pad_new_rows(ts, pairs):
    for pad_ref, new_ref in pairs:
        pad_ref[...] = jnp.zeros(pad_ref.shape, pad_ref.dtype)
        pad_ref[0:ts] = new_ref[...]


def _dsa_sample_kernel(ts, past, n_sel, q_ref, iq_ref, tail_ref, pk_ref, pv_ref, pik2_ref,
                       nk_ref, nv_ref, nik2_ref, o_ref,
                       key_ref, bias_ref, lhs_ref, iwb_ref, kpad_ref, vpad_ref, ikpad_ref):
    _broadcast_index_weights(tail_ref, iwb_ref, ts)
    _stack_index_queries(iq_ref, lhs_ref, ts)
    _pad_new_rows(ts, [(kpad_ref, nk_ref), (vpad_ref, nv_ref), (ikpad_ref, nik2_ref)])
    col = lax.broadcasted_iota(jnp.int32, (ts, LANES), 1)
    row = lax.broadcasted_iota(jnp.int32, (ts, LANES), 0)
    new_ok = (col < ts) & ((past + col) // CHUNK <= (past + row) // CHUNK)
    key_ref[...] = _sort_key(_index_scores(lhs_ref, iwb_ref, pik2_ref[0], ts))
    nkey = jnp.where(new_ok, _sort_key(_index_scores(lhs_ref, iwb_ref, ikpad_ref[...], ts)), INT_MIN)

    def count_ge(cand):
        return _lane_total(_count_block(key_ref[...], cand, jnp.where(nkey >= cand, 1.0, 0.0)))
    thr = _kth_largest_key(count_ge, ts, n_sel)
    bias_ref[...] = jnp.where(key_ref[...] >= _lane_tile(thr, past // LANES), 0.0, NEG)
    nbias = jnp.where(nkey >= thr, 0.0, NEG)

    for n in range(N_KV_HEADS):
        hs = slice(n * HEAD_DIM, (n + 1) * HEAD_DIM)
        _sample_head_attention(n, ts, q_ref, _cached_head(pk_ref, past, n), _cached_head(pv_ref, past, n),
                               kpad_ref[:, hs], vpad_ref[:, hs], lambda g: bias_ref[...], lambda g: nbias, o_ref)


def _flat_cache(cache):
    nl, b, past, kvh, hd = cache.shape
    return cache.reshape(nl, b, past * kvh, hd)


def _cache_spec(past, jmix):
    return pl.BlockSpec((1, 1, past * N_KV_HEADS, HEAD_DIM), lambda b: (jmix, b, 0, 0))


def _cached_head(c_ref, past, n):
    return c_ref[0, 0, pl.ds(n, past, stride=N_KV_HEADS), :].astype(BF16)


def _dsa_sample_call(q, iq, tail, cache_k, cache_v, jmix, pik2, nk, nv, nik2, batch, ts, past, n_sel):
    row_spec = lambda w: pl.BlockSpec((ts, w), lambda b: (b, 0))
    return pl.pallas_call(
        functools.partial(_dsa_sample_kernel, ts, past, n_sel),
        grid=(batch,),
        in_specs=[row_spec(Q_W), row_spec(IQ_W), row_spec(LANES),
                  _cache_spec(past, jmix), _cache_spec(past, jmix),
                  pl.BlockSpec((1, past, LANES), lambda b: (b, 0, 0)),
                  row_spec(KV_W), row_spec(KV_W), row_spec(LANES)],
        out_specs=row_spec(Q_W),
        out_shape=jax.ShapeDtypeStruct((batch * ts, Q_W), BF16),
        scratch_shapes=[pltpu.VMEM((ts, past), jnp.int32),
                        pltpu.VMEM((ts, past), F32),
                        pltpu.VMEM((IDX_HEADS * ts, LANES), BF16),
                        pltpu.VMEM((IDX_HEADS, ts, LANES), F32),
                        pltpu.VMEM((LANES, KV_W), BF16),
                        pltpu.VMEM((LANES, KV_W), BF16),
                        pltpu.VMEM((LANES, LANES), BF16)],
        compiler_params=_cparams(("arbitrary",)),
        name="dsa_sample",
    )(q, iq, tail, cache_k, cache_v, pik2, nk, nv, nik2)


def _fox_prompt_kernel(tq, tk, q_ref, k_ref, v_ref, cum_ref, o_ref, qg_ref, m_ref, l_ref, acc_ref):
    q0 = pl.multiple_of(pl.program_id(1) * tq, tq)
    nkb = (q0 + tq + tk - 1) // tk
    c0 = cum_ref[0, :, pl.ds(q0, LANES)][:, 0:1]
    row = lax.broadcasted_iota(jnp.int32, (tq, tk), 0)
    col = lax.broadcasted_iota(jnp.int32, (tq, tk), 1)
    for n in range(N_KV_HEADS):
        qg_ref[n] = _group_queries(q_ref, n)
    _flash_init(m_ref, l_ref, acc_ref)

    def body(kb, _):
        k0 = pl.multiple_of(kb * tk, tk)
        brow = (c0 - cum_ref[0, :, pl.ds(k0, tk)]) * LOG2E
        causal = k0 + col <= q0 + row
        for n in range(N_KV_HEADS):
            kt = k_ref[pl.ds(k0, tk), n * HEAD_DIM:(n + 1) * HEAD_DIM]
            vt = v_ref[pl.ds(k0, tk), n * HEAD_DIM:(n + 1) * HEAD_DIM]
            s = _nt_dot(qg_ref[n], kt)
            slabs = []
            for g in range(GROUP):
                h = n * GROUP + g
                slabs.append(jnp.where(causal, s[g * tq:(g + 1) * tq] + brow[h:h + 1, :], NEG))
            _flash_step(slabs, vt, tq, m_ref.at[n], l_ref.at[n], acc_ref.at[n])
        return 0
    lax.fori_loop(0, nkb, body, 0)
    for n in range(N_KV_HEADS):
        _flash_out(n, tq, l_ref.at[n], acc_ref.at[n], o_ref)


def _fox_prompt_call(q, kb16, vb16, cum, batch, t):
    tq = _pick(t, (128,))
    tk = _pick(t, (512, 256, 128))
    nq = t // tq
    row_spec = lambda w: pl.BlockSpec((tq, w), lambda b, i: (b * nq + i, 0))
    full_spec = lambda w: pl.BlockSpec((t, w), lambda b, i: (b, 0))
    return pl.pallas_call(
        functools.partial(_fox_prompt_kernel, tq, tk),
        grid=(batch, nq),
        in_specs=[row_spec(Q_W), full_spec(KV_W), full_spec(KV_W),
                  pl.BlockSpec((1, N_HEADS, t), lambda b, i: (b, 0, 0))],
        out_specs=row_spec(Q_W),
        out_shape=jax.ShapeDtypeStruct((batch * t, Q_W), BF16),
        scratch_shapes=_flash_scratch(tq),
        compiler_params=_cparams(("arbitrary", "arbitrary")),
        name="fox_prompt",
    )(q, kb16, vb16, cum)


def _fox_sample_kernel(ts, past, q_ref, pk_ref, pv_ref, nk_ref, nv_ref, cum_ref, o_ref, kpad_ref, vpad_ref):
    _pad_new_rows(ts, [(kpad_ref, nk_ref), (vpad_ref, nv_ref)])
    c0 = cum_ref[0, :, past - LANES:past][:, LANES - 1:LANES]
    brow_past = (c0 - cum_ref[0, :, 0:past]) * LOG2E
    brow_new = (c0 - cum_ref[0, :, past:past + LANES]) * LOG2E
    col = lax.broadcasted_iota(jnp.int32, (ts, LANES), 1)
    row = lax.broadcasted_iota(jnp.int32, (ts, LANES), 0)
    new_ok = col <= row
    for n in range(N_KV_HEADS):
        hs = slice(n * HEAD_DIM, (n + 1) * HEAD_DIM)
        head = lambda g: n * GROUP + g
        _sample_head_attention(
            n, ts, q_ref, _cached_head(pk_ref, past, n), _cached_head(pv_ref, past, n),
            kpad_ref[:, hs], vpad_ref[:, hs],
            lambda g: brow_past[head(g):head(g) + 1, :],
            lambda g: jnp.where(new_ok, brow_new[head(g):head(g) + 1, :], NEG), o_ref)


def _fox_sample_call(q, cache_k, cache_v, jmix, nk, nv, cum, batch, ts, past):
    lpad = cum.shape[2]
    row_spec = lambda w: pl.BlockSpec((ts, w), lambda b: (b, 0))
    return pl.pallas_call(
        functools.partial(_fox_sample_kernel, ts, past),
        grid=(batch,),
        in_specs=[row_spec(Q_W), _cache_spec(past, jmix), _cache_spec(past, jmix), row_spec(KV_W), row_spec(KV_W),
                  pl.BlockSpec((1, N_HEADS, lpad), lambda b: (b, 0, 0))],
        out_specs=row_spec(Q_W),
        out_shape=jax.ShapeDtypeStruct((batch * ts, Q_W), BF16),
        scratch_shapes=[pltpu.VMEM((LANES, KV_W), BF16),
                        pltpu.VMEM((LANES, KV_W), BF16)],
        compiler_params=_cparams(("arbitrary",)),
        name="fox_sample",
    )(q, cache_k, cache_v, nk, nv, cum)


HALO = 16


def _ffn_up_kernel(tm, bpb, h_ref, halo_ref, wg_ref, wv_ref, cwg_ref, cwv_ref, cbg_ref, cbv_ref,
                   a_ref, sg_ref, sv_ref, hbuf_ref, ug_ref, uv_ref):
    i, j = pl.program_id(0), pl.program_id(1)

    @pl.when(j == 0)
    def _():
        first = (i % bpb) == 0
        hbuf_ref[0:HALO] = jnp.where(first, jnp.zeros_like(halo_ref[...]), halo_ref[...])
        hbuf_ref[HALO:] = h_ref[...]

    hb = hbuf_ref[...]
    ug_ref[...] = jnp.dot(hb, wg_ref[0].astype(BF16), preferred_element_type=F32)
    uv_ref[...] = jnp.dot(hb, wv_ref[0].astype(BF16), preferred_element_type=F32)

    def conv(u_ref, cw_ref, cb_ref):
        z = cb_ref[0]
        for t in range(CONV_W):
            z = z + u_ref[HALO - (CONV_W - 1) + t:HALO - (CONV_W - 1) + t + tm] * cw_ref[0, t:t + 1, :]
        return z
    zg = conv(ug_ref, cwg_ref, cbg_ref)
    zv = conv(uv_ref, cwv_ref, cbv_ref)
    a_ref[...] = (zg * jax.nn.sigmoid(zg) * zv).astype(a_ref.dtype)
    sg_ref[0] = ug_ref[HALO + tm - (CONV_W - 1):HALO + tm]
    sv_ref[0] = uv_ref[HALO + tm - (CONV_W - 1):HALO + tm]


def _ffn_up_call(h, w_up, conv_w, conv_b3, l, batch, t):
    m, d = h.shape
    dff = w_up.shape[2] // 2
    tm = _pick(t, (1024, 512, 256, 128))
    tn = _pick(dff, (512, 256, 128))
    ncb = dff // tn
    bpb = t // tm
    hpb = tm // HALO
    st = jax.ShapeDtypeStruct((m // tm, CONV_W - 1, dff), F32)
    st_spec = pl.BlockSpec((1, CONV_W - 1, tn), lambda i, j: (i, 0, j))
    a, sg, sv = pl.pallas_call(
        functools.partial(_ffn_up_kernel, tm, bpb),
        grid=(m // tm, ncb),
        in_specs=[pl.BlockSpec((tm, d), lambda i, j: (i, 0)),
                  pl.BlockSpec((HALO, d), lambda i, j: (jnp.maximum(i * hpb - 1, 0), 0)),
                  pl.BlockSpec((1, d, tn), lambda i, j: (l, 0, j)),
                  pl.BlockSpec((1, d, tn), lambda i, j: (l, 0, ncb + j)),
                  pl.BlockSpec((1, CONV_W, tn), lambda i, j: (l, 0, j)),
                  pl.BlockSpec((1, CONV_W, tn), lambda i, j: (l, 0, ncb + j)),
                  pl.BlockSpec((1, 1, tn), lambda i, j: (l, 0, j)),
                  pl.BlockSpec((1, 1, tn), lambda i, j: (l, 0, ncb + j))],
        out_specs=[pl.BlockSpec((tm, tn), lambda i, j: (i, j)), st_spec, st_spec],
        out_shape=[jax.ShapeDtypeStruct((m, dff), BF16), st, st],
        scratch_shapes=[pltpu.VMEM((HALO + tm, d), BF16),
                        pltpu.VMEM((HALO + tm, tn), F32),
                        pltpu.VMEM((HALO + tm, tn), F32)],
        compiler_params=_cparams(("arbitrary", "arbitrary")),
        name="ffn_up_conv_gate",
    )(h, h, w_up, w_up, conv_w, conv_w, conv_b3, conv_b3)
    return a, sg[bpb - 1::bpb], sv[bpb - 1::bpb]


def _conv_gate_kernel(u0g, u1g, u2g, u0v, u1v, u2v, cwg_ref, cwv_ref, cbg_ref, cbv_ref, a_ref):
    zg = cbg_ref[0] + u0g[...] * cwg_ref[0, 0:1, :] + u1g[...] * cwg_ref[0, 1:2, :] + u2g[...] * cwg_ref[0, 2:3, :]
    zv = cbv_ref[0] + u0v[...] * cwv_ref[0, 0:1, :] + u1v[...] * cwv_ref[0, 1:2, :] + u2v[...] * cwv_ref[0, 2:3, :]
    a_ref[...] = (zg * jax.nn.sigmoid(zg) * zv).astype(a_ref.dtype)


def _conv_gate_call(shifted, conv_w, conv_b3, l):
    m, n2 = shifted[0].shape
    dff = n2 // 2
    tn = _pick(dff, (512, 256, 128))
    ncb = dff // tn
    g_spec = pl.BlockSpec((m, tn), lambda j: (0, j))
    v_spec = pl.BlockSpec((m, tn), lambda j: (0, ncb + j))
    return pl.pallas_call(
        _conv_gate_kernel,
        grid=(ncb,),
        in_specs=[g_spec, g_spec, g_spec, v_spec, v_spec, v_spec,
                  pl.BlockSpec((1, CONV_W, tn), lambda j: (l, 0, j)),
                  pl.BlockSpec((1, CONV_W, tn), lambda j: (l, 0, ncb + j)),
                  pl.BlockSpec((1, 1, tn), lambda j: (l, 0, j)),
                  pl.BlockSpec((1, 1, tn), lambda j: (l, 0, ncb + j))],
        out_specs=pl.BlockSpec((m, tn), lambda j: (0, j)),
        out_shape=jax.ShapeDtypeStruct((m, dff), BF16),
        compiler_params=_cparams(("arbitrary",)),
        name="conv_gate",
    )(*shifted, *shifted, conv_w, conv_w, conv_b3, conv_b3)


def _trunk(x, mods, batch, t, past, caches, w):
    m, d = x.shape
    depth = w['norm_mix_g'].shape[0]
    tm = _pick(m, (1024, 512, 256, 128))
    length = past + t
    n_sel = min(TOPK_MAX, length // 4)
    pos = (past + jnp.arange(t, dtype=jnp.int32)).astype(F32)
    if t % tm == 0:
        rep_rows = lambda tab: tab
        tab_spec = lambda: pl.BlockSpec((tm, 3 * LANES), lambda i, j, k: (i % (t // tm), 0))
    else:
        rep_rows = lambda tab: jnp.tile(tab, (m // t, 1))
        tab_spec = lambda: pl.BlockSpec((tm, 3 * LANES), lambda i, j, k: (i, 0))
    scale = HEAD_DIM ** -0.5 * LOG2E

    def tail_lanes(a, bm, bp):
        lane = np.arange(LANES)
        is_iw = jnp.asarray((lane >= IDX_DIM) & (lane < IDX_DIM + IDX_HEADS))[None, :]
        return jnp.where(is_iw, IDX_HEADS ** -0.5 * IDX_DIM ** -0.5, a), bm, bp

    def tail_weights(w_in, c0, ncols):
        return jnp.pad(w_in[:, :, c0:c0 + ncols], ((0, 0), (0, 0), (0, LANES - ncols)))

    tab_q = rep_rows(_rope_table(pos, ROT_DIM, HEAD_DIM, scale))
    tab_k = rep_rows(_rope_table(pos, ROT_DIM, HEAD_DIM))
    tab_iq = rep_rows(_rope_table(pos, IDX_ROT_DIM, IDX_DIM))
    tab_tail = rep_rows(_rope_table(pos, IDX_ROT_DIM, LANES, const_lanes=tail_lanes))

    st = dict(ka=[], va=[], ika=[], kb=[], vb=[], lfb=[], us=[])
    for l in range(depth):
        sh_m, sc_m, g_m, sh_f, sc_f, g_f = mods[l]
        jmix = l // N_MIXERS
        h = _norm_call(x, w['norm_mix_g3'], l, sc_m, sh_m, BF16)
        lin = functools.partial(_linear_call, a=h, tm=tm, tk=d)
        if l % N_MIXERS == 0:
            w_in = w['dsa_w_in']
            tn = 512
            (q,) = lin("dsa_q", functools.partial(_ep_rope, ROT_DIM // 2), w3=w_in, l=jmix, col0=0, ncols=Q_W,
                       tn=tn, extras=[(tab_q, tab_spec())], out_dtypes=[BF16])
            k32, k16 = lin("dsa_k", functools.partial(_ep_rope, ROT_DIM // 2), w3=w_in, l=jmix, col0=Q_W,
                           ncols=KV_W, tn=tn, extras=[(tab_k, tab_spec())], out_dtypes=[F32, BF16], head_flat=1)
            v32, v16 = lin("dsa_v", _ep_plain, w3=w_in, l=jmix, col0=Q_W + KV_W, ncols=KV_W, tn=tn,
                           extras=[], out_dtypes=[F32, BF16], head_flat=1)
            (iq,) = lin("dsa_iq", functools.partial(_ep_rope, IDX_ROT_DIM // 2), w3=w_in, l=jmix,
                        col0=Q_W + 2 * KV_W, ncols=IQ_W, tn=tn, extras=[(tab_iq, tab_spec())], out_dtypes=[BF16])
            w_tail = tail_weights(w_in, Q_W + 2 * KV_W + IQ_W, IDX_DIM + IDX_HEADS)
            (tail,) = lin("dsa_tail", functools.partial(_ep_rope, IDX_ROT_DIM // 2), w3=w_tail, l=jmix,
                          col0=0, ncols=LANES, tn=LANES, extras=[(tab_tail, tab_spec())], out_dtypes=[F32])
            ik = tail[:, :IDX_DIM]
            ik2 = jnp.concatenate([ik, ik], axis=1).astype(BF16)
            if caches is None:
                attn = _dsa_prompt_call(q, iq, tail, k16, v16, ik2, batch, t, n_sel)
            else:
                pik = caches['dsa_ik'][jmix]
                pik2 = jnp.concatenate([pik, pik], axis=-1).astype(BF16)
                attn = _dsa_sample_call(q, iq, tail, _flat_cache(caches['dsa_k']), _flat_cache(caches['dsa_v']), jmix,
                                        pik2, k16, v16, ik2, batch, t, past, n_sel)
            st['ka'].append(k32.reshape(batch, t, N_KV_HEADS, HEAD_DIM))
            st['va'].append(v32.reshape(batch, t, N_KV_HEADS, HEAD_DIM))
            st['ika'].append(ik.reshape(batch, t, IDX_DIM))
            w_o = w['dsa_w_o']
        else:
            w_in = w['fox_w_in']
            tn = 512
            (q,) = lin("fox_q", functools.partial(_ep_scale, scale), w3=w_in, l=jmix, col0=0, ncols=Q_W, tn=tn,
                       extras=[], out_dtypes=[BF16])
            k32, k16 = lin("fox_k", _ep_plain, w3=w_in, l=jmix, col0=Q_W, ncols=KV_W, tn=tn, extras=[],
                           out_dtypes=[F32, BF16], head_flat=1)
            v32, v16 = lin("fox_v", _ep_plain, w3=w_in, l=jmix, col0=Q_W + KV_W, ncols=KV_W, tn=tn, extras=[],
                           out_dtypes=[F32, BF16], head_flat=1)
            bf = jnp.pad(w['fox_b_f'][jmix], (0, LANES - N_HEADS)).reshape(1, LANES)
            w_tail = tail_weights(w_in, Q_W + 2 * KV_W, N_HEADS)
            (lf_tail,) = lin("fox_logf", _ep_logsig, w3=w_tail, l=jmix, col0=0, ncols=LANES, tn=LANES,
                             extras=[(bf, pl.BlockSpec((1, LANES), lambda i, j, k: (0, 0)))], out_dtypes=[F32])
            logf = lf_tail[:, :N_HEADS].reshape(batch, t, N_HEADS)
            if caches is None:
                lf_t = jnp.swapaxes(logf, 1, 2).reshape(batch * N_HEADS, t)
                cum = _cumsum_call(lf_t).reshape(batch, N_HEADS, t)
                attn = _fox_prompt_call(q, k16, v16, cum, batch, t)
            else:
                lf_all = jnp.concatenate([caches['fox_lf'][jmix], logf], axis=1)
                lpad = past + LANES
                lf_t = jnp.pad(jnp.swapaxes(lf_all, 1, 2), ((0, 0), (0, 0), (0, lpad - length)))
                cum = _cumsum_call(lf_t.reshape(batch * N_HEADS, lpad)).reshape(batch, N_HEADS, lpad)
                attn = _fox_sample_call(q, _flat_cache(caches['fox_k']), _flat_cache(caches['fox_v']), jmix,
                                        k16, v16, cum, batch, t, past)
            st['kb'].append(k32.reshape(batch, t, N_KV_HEADS, HEAD_DIM))
            st['vb'].append(v32.reshape(batch, t, N_KV_HEADS, HEAD_DIM))
            st['lfb'].append(logf)
            w_o = w['fox_w_o']

        res_extras = lambda xx, gg, tn_o: [(xx, pl.BlockSpec((tm, tn_o), lambda i, j, k: (i, j))),
                                           (gg, _mod_spec(gg, m, tm, tn_o, lambda i, j, k: j))]
        (x,) = _linear_call("attn_out", _ep_residual, attn, w_o, jmix, 0, d, 512, tm, Q_W,
                            res_extras(x, g_m, 512), [F32])

        h = _norm_call(x, w['norm_ffn_g3'], l, sc_f, sh_f, BF16)
        dff = w['ffn_w_down'].shape[1]
        if caches is None:
            a, sg, sv = _ffn_up_call(h, w['ffn_w_up'], w['ffn_conv_w'], w['ffn_conv_b3'], l, batch, t)
            st['us'].append(jnp.concatenate([sg, sv], axis=-1))
        else:
            (u,) = _linear_call("ffn_up", _ep_plain, h, w['ffn_w_up'], l, 0, 2 * dff, 512, tm, d, [], [F32])
            u_all = jnp.concatenate([caches['ffn_u'][l], u.reshape(batch, t, 2 * dff)], axis=1)
            shifted = [u_all[:, s:s + t].reshape(m, 2 * dff) for s in range(CONV_W)]
            a = _conv_gate_call(shifted, w['ffn_conv_w'], w['ffn_conv_b3'], l)
            st['us'].append(u_all[:, t:])
        tk_d = _pick(dff, (1408, 1024, 512))
        (x,) = _linear_call("ffn_down", _ep_residual, a, w['ffn_w_down'], l, 0, d, 1024, tm, tk_d,
                            res_extras(x, g_f, 1024), [F32])

    y = _norm_call(x, w['final_g3'], 0, None, None, F32)
    return y, st


def kernel(x_prompt, x_sample, c_prompt, c_sample, cache_dsa_k, cache_dsa_v, cache_dsa_idx_k, cache_fox_k,
           cache_fox_v, cache_fox_logf, state_ffn_conv, norm_mix_g, norm_ffn_g, w_ada, b_ada, dsa_w_in, dsa_w_o,
           fox_w_in, fox_b_f, fox_w_o, ffn_w_up, ffn_conv_w, ffn_conv_b, ffn_w_down, final_g):
    b, t, d = x_prompt.shape
    bs, ts, _ = x_sample.shape
    past = cache_dsa_k.shape[2]
    depth = norm_mix_g.shape[0]
    w = dict(norm_mix_g=norm_mix_g, norm_mix_g3=norm_mix_g.reshape(depth, 1, d),
             norm_ffn_g3=norm_ffn_g.reshape(depth, 1, d), final_g3=final_g.reshape(1, 1, d),
             dsa_w_in=dsa_w_in, dsa_w_o=dsa_w_o, fox_w_in=fox_w_in, fox_b_f=fox_b_f, fox_w_o=fox_w_o,
             ffn_w_up=ffn_w_up, ffn_conv_w=ffn_conv_w, ffn_conv_b3=ffn_conv_b.reshape(depth, 1, -1),
             ffn_w_down=ffn_w_down)

    nc = b + bs
    nc_pad = -(-nc // 8) * 8
    c_all = jnp.pad(jnp.concatenate([c_prompt, c_sample], axis=0), ((0, nc_pad - nc), (0, 0)))
    mod = _ada_call(c_all, w_ada, b_ada)
    mods_p, mods_s = [], []
    for l in range(depth):
        parts = [mod[l, :, i * d:(i + 1) * d] for i in range(6)]
        mods_p.append([p[:b].reshape(b, 1, d) for p in parts])
        mods_s.append([jnp.repeat(p[b:nc], ts, axis=0).reshape(1, bs * ts, d) for p in parts])

    y_p, sp = _trunk(x_prompt.reshape(b * t, d), mods_p, b, t, 0, None, w)
    caches = dict(dsa_k=cache_dsa_k, dsa_v=cache_dsa_v, dsa_ik=cache_dsa_idx_k, fox_k=cache_fox_k,
                  fox_v=cache_fox_v, fox_lf=cache_fox_logf, ffn_u=state_ffn_conv)
    y_s, ss = _trunk(x_sample.reshape(bs * ts, d), mods_s, bs, ts, past, caches, w)

    names = ('ka', 'va', 'ika', 'kb', 'vb', 'lfb', 'us')
    return ((y_p.reshape(b, t, d), y_s.reshape(bs, ts, d))
            + tuple(jnp.stack(sp[n]) for n in names) + tuple(jnp.stack(ss[n]) for n in names))
```
